```python
import jax, jax.numpy as jnp
from jax import lax
import numpy as np

D_MODEL = 4096
BATCH = 4
SEQ = 2048
DEPTH = 2
DEC_BATCH = 8
DEC_SEQ = 4
PAST_LEN = 16384
PAGE_SIZE = 128

N_A_LAYERS = DEPTH // 2
N_B_LAYERS = DEPTH - N_A_LAYERS
A_HEADS = 8
A_QK_DIM = D_MODEL // 2 // A_HEADS
A_V_DIM = D_MODEL // A_HEADS
MLSTM_CHUNK = 64
A_QK = A_HEADS * A_QK_DIM
A_VD = A_HEADS * A_V_DIM
A_PROJ = 2 * A_QK + A_VD + D_MODEL + 2 * A_HEADS
F_GATE_BIAS = 3.0
HEAD_DIM = 128
B_HEADS = D_MODEL // HEAD_DIM
KV_HEADS = 8
Q_PER_KV = B_HEADS // KV_HEADS
SB_BLOCK = 128
SB_LOGIT_BIAS = -8.0
N_GROUPS = 4
EXPERTS_PER_GROUP = 8
N_EXPERTS = N_GROUPS * EXPERTS_PER_GROUP
TOP_K = 2
D_EXPERT = D_MODEL // 4
RMS_EPS = 1e-6

kernel_name = "yoco_mlstm_stickbreaking_hmoe_step"


def rmsnorm(x, g):
    xf = x.astype(jnp.float32)
    y = xf * lax.rsqrt(jnp.mean(xf * xf, axis=-1, keepdims=True) + RMS_EPS) * g.astype(jnp.float32)
    return y.astype(x.dtype)


def mlstm_chunked(q, k, v, i_pre, logf, C0, n0, m0):
    B, H, T, _ = q.shape
    L = MLSTM_CHUNK if T % MLSTM_CHUNK == 0 else T
    nc = T // L

    def split(a):
        return jnp.moveaxis(a.reshape((B, H, nc, L) + a.shape[3:]), 2, 0)

    causal = jnp.tril(jnp.ones((L, L), dtype=bool))

    def step(carry, inp):
        C, n, m = carry
        qc, kc, vc, ic, fc = inp
        b = jnp.cumsum(fc, axis=-1)
        dmat = jnp.where(causal, b[..., :, None] - b[..., None, :] + ic[..., None, :], -jnp.inf)
        inter = b + m[..., None]
        m_t = jnp.maximum(inter, jnp.max(dmat, axis=-1))
        w_intra = jnp.exp(dmat - m_t[..., None])
        w_inter = jnp.exp(inter - m_t)
        s = jnp.einsum('bhtd,bhsd->bhts', qc, kc) * w_intra
        num = w_inter[..., None] * jnp.einsum('bhtd,bhde->bhte', qc, C) + jnp.einsum('bhts,bhse->bhte', s, vc)
        den = w_inter * jnp.einsum('bhtd,bhd->bht', qc, n) + jnp.sum(s, axis=-1)
        h = num / jnp.maximum(jnp.abs(den), jnp.exp(-m_t))[..., None]
        bL = b[..., -1]
        g = bL[..., None] - b + ic
        m_new = jnp.maximum(bL + m, jnp.max(g, axis=-1))
        wk = jnp.exp(g - m_new[..., None])
        decay = jnp.exp(bL + m - m_new)
        C_new = decay[..., None, None] * C + jnp.einsum('bhsd,bhse->bhde', kc * wk[..., None], vc)
        n_new = decay[..., None] * n + jnp.einsum('bhs,bhsd->bhd', wk, kc)
        return (C_new, n_new, m_new), h

    (C, n, m), hs = lax.scan(step, (C0, n0, m0), (split(q), split(k), split(v), split(i_pre), split(logf)))
    h = jnp.moveaxis(hs, 0, 2).reshape(B, H, T, -1)
    return h, C, n, m


def mlstm_mixer(xn, C0, n0, m0, w_in, b_gate, g_h, w_out):
    B, T, _ = xn.shape
    p = xn @ w_in
    q, k, v, o, gates = jnp.split(p, [A_QK, 2 * A_QK, 2 * A_QK + A_VD, 2 * A_QK + A_VD + D_MODEL], axis=-1)

    def heads(a, d):
        return a.reshape(B, T, A_HEADS, d).transpose(0, 2, 1, 3).astype(jnp.float32)

    q = heads(q, A_QK_DIM) * (A_QK_DIM ** -0.5)
    k = heads(k, A_QK_DIM)
    v = heads(v, A_V_DIM)
    gates = (gates.astype(jnp.float32) + b_gate.astype(jnp.float32)).transpose(0, 2, 1)
    i_pre = gates[:, :A_HEADS]
    logf = jax.nn.log_sigmoid(gates[:, A_HEADS:])
    h, C, n, m = mlstm_chunked(q, k, v, i_pre, logf, C0.astype(jnp.float32), n0.astype(jnp.float32), m0.astype(jnp.float32))
    h = rmsnorm(h.transpose(0, 2, 1, 3), g_h)
    h = h.reshape(B, T, A_VD) * jax.nn.sigmoid(o.astype(jnp.float32))
    return h.astype(xn.dtype) @ w_out, C, n, m


def shared_kv(h, g_kv, w_kv, g_knorm):
    B, T, _ = h.shape
    k, v = jnp.split(rmsnorm(h, g_kv) @ w_kv, 2, axis=-1)
    k = rmsnorm(k.reshape(B, T, KV_HEADS, HEAD_DIM), g_knorm)
    v = v.reshape(B, T, KV_HEADS, HEAD_DIM)
    return k, v


def stick_breaking(q, k, v, b_logit, q_pos0):
    B, Tq = q.shape[:2]
    Tk = k.shape[1]
    qg = q.reshape(B, Tq, KV_HEADS, Q_PER_KV, HEAD_DIM).astype(jnp.float32)
    kf = k.astype(jnp.float32)
    vf = v.astype(jnp.float32)
    bias = b_logit.astype(jnp.float32).reshape(1, KV_HEADS, Q_PER_KV, 1, 1)
    key_pos = jnp.arange(Tk)
    q_pos = q_pos0 + jnp.arange(Tq)
    scale = HEAD_DIM ** -0.5

    def block(args):
        qb, pos = args
        z = jnp.einsum('bqkgd,bskd->bkgqs', qb, kf) * scale + bias
        mask = key_pos[None, :] < pos[:, None]
        u = jnp.where(mask, jax.nn.log_sigmoid(-z), 0.0)
        between = lax.cumsum(u, axis=4, reverse=True) - u
        a = jnp.where(mask, jnp.exp(jax.nn.log_sigmoid(z) + between), 0.0)
        return jnp.einsum('bkgqs,bskd->bqkgd', a, vf)

    if Tq > SB_BLOCK and Tq % SB_BLOCK == 0:
        nb = Tq // SB_BLOCK
        qb = jnp.moveaxis(qg.reshape(B, nb, SB_BLOCK, KV_HEADS, Q_PER_KV, HEAD_DIM), 1, 0)
        out = lax.map(block, (qb, q_pos.reshape(nb, SB_BLOCK)))
        out = jnp.moveaxis(out, 0, 1).reshape(B, Tq, KV_HEADS, Q_PER_KV, HEAD_DIM)
    else:
        out = block((qg, q_pos))
    return out


def sb_mixer(xn, k_all, v_all, q_pos0, w_q, g_q, b_logit, w_o):
    B, T, _ = xn.shape
    q = rmsnorm((xn @ w_q).reshape(B, T, B_HEADS, HEAD_DIM), g_q)
    o = stick_breaking(q, k_all, v_all, b_logit, q_pos0)
    return o.reshape(B, T, B_HEADS * HEAD_DIM).astype(xn.dtype) @ w_o


def grouped_ffn(x, expert_idx, gates, w_in, w_out):
    T = x.shape[0]
    M = T * TOP_K
    blk = 128 if M >= 128 * N_EXPERTS else 8
    flat_e = expert_idx.reshape(M)
    order = jnp.argsort(flat_e)
    sorted_e = flat_e[order]
    tok = order // TOP_K
    counts = jnp.bincount(flat_e, length=N_EXPERTS)
    padded = ((counts + blk - 1) // blk) * blk
    ends = jnp.cumsum(padded)
    pad_start = ends - padded
    start = jnp.cumsum(counts) - counts
    dest = pad_start[sorted_e] + (jnp.arange(M) - start[sorted_e])
    n_blocks = -(-M // blk) + N_EXPERTS
    row_tok = jnp.zeros((n_blocks * blk,), jnp.int32).at[dest].set(tok.astype(jnp.int32))
    block_expert = jnp.minimum(jnp.searchsorted(ends, jnp.arange(n_blocks) * blk, side='right'), N_EXPERTS - 1)
    xs = x[row_tok].reshape(n_blocks, blk, x.shape[-1])

    def expert_block(args):
        xb, e = args
        g, u = jnp.split(xb @ w_in[e], 2, axis=-1)
        return (jax.nn.silu(g) * u) @ w_out[e]

    ys = lax.map(expert_block, (xs, block_expert)).reshape(n_blocks * blk, -1)
    contrib = ys[dest].astype(jnp.float32) * gates.reshape(M)[order][:, None]
    return jnp.zeros(x.shape, jnp.float32).at[tok].add(contrib).astype(x.dtype)


def hier_moe(x, w_group, w_router, w_in, w_out):
    T = x.shape[0]
    p_group = jax.nn.softmax((x @ w_group).astype(jnp.float32), axis=-1)
    grp = jnp.argmax(p_group, axis=-1)
    g_gate = jnp.take_along_axis(p_group, grp[:, None], axis=-1)
    e_logits = (x @ w_router).astype(jnp.float32).reshape(T, N_GROUPS, EXPERTS_PER_GROUP)
    e_logits = jnp.take_along_axis(e_logits, grp[:, None, None], axis=1)[:, 0]
    top_v, top_i = lax.top_k(e_logits, TOP_K)
    gates = jax.nn.softmax(top_v, axis=-1) * g_gate
    expert_idx = grp[:, None] * EXPERTS_PER_GROUP + top_i
    return grouped_ffn(x, expert_idx, gates, w_in, w_out)


def trunk(x, a_states, past_k, past_v, q_pos0, params):
    (norm_mix, w_in_a, b_gate_a, g_hnorm_a, w_out_a, g_kv, w_kv, g_knorm,
     w_q_b, g_qnorm_b, b_logit_b, w_out_b, norm_ffn, w_group, w_router, w_moe_in, w_moe_out) = params
    B, T, D = x.shape
    h = x
    new_states = []
    k_new = v_new = k_all = v_all = None
    for l in range(DEPTH):
        xn = rmsnorm(h, norm_mix[l])
        if l < N_A_LAYERS:
            C0, n0, m0 = a_states[l]
            y, C, n, m = mlstm_mixer(xn, C0, n0, m0, w_in_a[l], b_gate_a[l], g_hnorm_a[l], w_out_a[l])
            new_states.append((C, n, m))
        else:
            if l == N_A_LAYERS:
                k_new, v_new = shared_kv(h, g_kv, w_kv, g_knorm)
                if past_k is None:
                    k_all, v_all = k_new, v_new
                else:
                    k_all = jnp.concatenate([past_k.astype(k_new.dtype), k_new], axis=1)
                    v_all = jnp.concatenate([past_v.astype(v_new.dtype), v_new], axis=1)
            j = l - N_A_LAYERS
            y = sb_mixer(xn, k_all, v_all, q_pos0, w_q_b[j], g_qnorm_b[j], b_logit_b[j], w_out_b[j])
        h = h + y
        f = hier_moe(rmsnorm(h, norm_ffn[l]).reshape(B * T, D), w_group[l], w_router[l], w_moe_in[l], w_moe_out[l])
        h = h + f.reshape(B, T, D)
    return h, new_states, k_new, v_new


def setup_inputs(seed: int = 0) -> dict:
    key = jax.random.key(seed)
    ks = iter(jax.random.split(key, 40))

    def nrm(shape, scale):
        return jax.random.normal(next(ks), shape, jnp.float32) * scale

    def gain(shape):
        return 1.0 + nrm(shape, 0.02)

    n_pages = PAST_LEN // PAGE_SIZE
    n_used = DEC_BATCH * n_pages
    n_phys = n_used + max(1, n_used // 4)
    inp = {}
    inp['x_prompt'] = nrm((BATCH, SEQ, D_MODEL), 1.0)
    inp['x_sample'] = nrm((DEC_BATCH, DEC_SEQ, D_MODEL), 1.0)
    inp['state_C'] = nrm((N_A_LAYERS, DEC_BATCH, A_HEADS, A_QK_DIM, A_V_DIM), 0.05)
    inp['state_n'] = nrm((N_A_LAYERS, DEC_BATCH, A_HEADS, A_QK_DIM), 0.5)
    inp['state_m'] = nrm((N_A_LAYERS, DEC_BATCH, A_HEADS), 1.0)
    inp['cache_k'] = nrm((n_phys, PAGE_SIZE, KV_HEADS, HEAD_DIM), 1.0)
    inp['cache_v'] = nrm((n_phys, PAGE_SIZE, KV_HEADS, HEAD_DIM), 1.0)
    inp['page_table'] = jax.random.permutation(next(ks), n_phys)[:n_used].reshape(DEC_BATCH, n_pages).astype(jnp.int32)
    inp['norm_mix'] = gain((DEPTH, D_MODEL))
    inp['w_in_a'] = nrm((N_A_LAYERS, D_MODEL, A_PROJ), D_MODEL ** -0.5)
    inp['b_gate_a'] = jnp.concatenate([nrm((N_A_LAYERS, A_HEADS), 0.1),
                                       F_GATE_BIAS + nrm((N_A_LAYERS, A_HEADS), 0.5)], axis=-1)
    inp['g_hnorm_a'] = gain((N_A_LAYERS, A_HEADS, A_V_DIM))
    inp['w_out_a'] = nrm((N_A_LAYERS, A_VD, D_MODEL), A_VD ** -0.5)
    inp['g_kv'] = gain((D_MODEL,))
    inp['w_kv'] = nrm((D_MODEL, 2 * KV_HEADS * HEAD_DIM), D_MODEL ** -0.5)
    inp['g_knorm'] = gain((HEAD_DIM,))
    inp['w_q_b'] = nrm((N_B_LAYERS, D_MODEL, B_HEADS * HEAD_DIM), D_MODEL ** -0.5)
    inp['g_qnorm_b'] = gain((N_B_LAYERS, HEAD_DIM))
    inp['b_logit_b'] = SB_LOGIT_BIAS + nrm((N_B_LAYERS, B_HEADS), 0.3)
    inp['w_out_b'] = nrm((N_B_LAYERS, B_HEADS * HEAD_DIM, D_MODEL), (B_HEADS * HEAD_DIM) ** -0.5)
    inp['norm_ffn'] = gain((DEPTH, D_MODEL))
    inp['w_group'] = nrm((DEPTH, D_MODEL, N_GROUPS), D_MODEL ** -0.5)
    inp['w_router'] = nrm((DEPTH, D_MODEL, N_EXPERTS), D_MODEL ** -0.5)
    inp['w_moe_in'] = nrm((DEPTH, N_EXPERTS, D_MODEL, 2 * D_EXPERT), D_MODEL ** -0.5)
    inp['w_moe_out'] = nrm((DEPTH, N_EXPERTS, D_EXPERT, D_MODEL), D_EXPERT ** -0.5)
    return inp


def reference(x_prompt, x_sample, state_C, state_n, state_m, cache_k, cache_v, page_table,
              norm_mix, w_in_a, b_gate_a, g_hnorm_a, w_out_a, g_kv, w_kv, g_knorm,
              w_q_b, g_qnorm_b, b_logit_b, w_out_b, norm_ffn, w_group, w_router, w_moe_in, w_moe_out):
    params = (norm_mix, w_in_a, b_gate_a, g_hnorm_a, w_out_a, g_kv, w_kv, g_knorm,
              w_q_b, g_qnorm_b, b_logit_b, w_out_b, norm_ffn, w_group, w_router, w_moe_in, w_moe_out)
    bp = x_prompt.shape[0]
    zero_states = [(jnp.zeros((bp, A_HEADS, A_QK_DIM, A_V_DIM), jnp.float32),
                    jnp.zeros((bp, A_HEADS, A_QK_DIM), jnp.float32),
                    jnp.zeros((bp, A_HEADS), jnp.float32)) for _ in range(N_A_LAYERS)]
    y_prompt, st_p, prompt_k, prompt_v = trunk(x_prompt, zero_states, None, None, 0, params)
    n_seq, n_pages = page_table.shape
    past_len = n_pages * cache_k.shape[1]
    past_k = cache_k[page_table].reshape(n_seq, past_len, KV_HEADS, HEAD_DIM)
    past_v = cache_v[page_table].reshape(n_seq, past_len, KV_HEADS, HEAD_DIM)
    carried = [(state_C[l], state_n[l], state_m[l]) for l in range(N_A_LAYERS)]
    y_sample, st_s, sample_k, sample_v = trunk(x_sample, carried, past_k, past_v, past_len, params)
    prompt_C = jnp.stack([s[0] for s in st_p])
    prompt_n = jnp.stack([s[1] for s in st_p])
    prompt_m = jnp.stack([s[2] for s in st_p])
    sample_C = jnp.stack([s[0] for s in st_s])
    sample_n = jnp.stack([s[1] for s in st_s])
    sample_m = jnp.stack([s[2] for s in st_s])
    return (y_prompt, y_sample, prompt_C, prompt_n, prompt_m, prompt_k, prompt_v,
            sample_C, sample_n, sample_m, sample_k, sample_v)
```

```python
import functools

import jax
import jax.numpy as jnp
from jax import lax
from jax.experimental import pallas as pl
from jax.experimental.pallas import tpu as pltpu

F32 = jnp.float32
BF16 = jnp.bfloat16

D_MODEL = 4096
BATCH = 4
SEQ = 2048
DEC_BATCH = 8
DEC_SEQ = 4
PAGE_SIZE = 128
A_HEADS = 8
A_QK_DIM = 256
A_V_DIM = 512
A_QK = A_HEADS * A_QK_DIM
A_VD = A_HEADS * A_V_DIM
A_MAIN = 2 * A_QK + A_VD + D_MODEL
HEAD_DIM = 128
B_HEADS = 32
KV_HEADS = 8
Q_PER_KV = B_HEADS // KV_HEADS
N_GROUPS = 4
EXPERTS_PER_GROUP = 8
N_EXPERTS = N_GROUPS * EXPERTS_PER_GROUP
D_EXPERT = 1024
RMS_EPS = 1e-6

LANES = 128
MP = BATCH * SEQ
S_PAD = 16
MS = DEC_BATCH * S_PAD
M_ALL = MP + MS
NEG_BIG = -1e30
VMEM_LIMIT = 56 * 1024 * 1024

HIGHEST = lax.Precision.HIGHEST


def _cparams(*sem):
    return pltpu.CompilerParams(dimension_semantics=sem, vmem_limit_bytes=VMEM_LIMIT)


def _dot(a, b, precision=None):
    return jnp.dot(a, b, preferred_element_type=F32, precision=precision)


def _dot_nt(a, b):
    return lax.dot_general(a, b, (((1,), (1,)), ((), ())), preferred_element_type=F32)


def _log_sigmoid(x):
    return jnp.minimum(x, 0.0) - jnp.log1p(jnp.exp(-jnp.abs(x)))


def _rms_body(x_ref, g_ref, *out_refs):
    x = x_ref[...].astype(F32)
    xr = x * lax.rsqrt(jnp.mean(x * x, axis=-1, keepdims=True) + RMS_EPS)
    for j, o_ref in enumerate(out_refs):
        o_ref[...] = (xr * g_ref[j:j + 1, :]).astype(o_ref.dtype)


def rms_norm(x, gains, out_dtypes, tr):
    m, d = x.shape
    n = len(out_dtypes)
    row = pl.BlockSpec((tr, d), lambda i: (i, 0))
    return pl.pallas_call(
        _rms_body, grid=(m // tr,),
        in_specs=[row, pl.BlockSpec((n, d), lambda i: (0, 0))],
        out_specs=[row] * n,
        out_shape=[jax.ShapeDtypeStruct((m, d), dt) for dt in out_dtypes],
        compiler_params=_cparams("arbitrary"), name="rms_norm")(x, gains)


def _mm_body(x_ref, w_ref, *rest, mode, n_norm_tiles):
    if mode == "resid":
        r_ref, o_ref, wb_ref = rest
    elif mode == "headnorm":
        g_ref, o_ref, wb_ref = rest
    else:
        o_ref, wb_ref = rest
    j = pl.program_id(0)

    @pl.when(pl.program_id(1) == 0)
    def _():
        wb_ref[...] = w_ref[...].astype(BF16)

    acc = _dot(x_ref[...], wb_ref[...])
    if mode == "plain":
        o_ref[...] = acc.astype(o_ref.dtype)
    elif mode == "resid":
        o_ref[...] = r_ref[...] + acc
    else:
        @pl.when(j < n_norm_tiles)
        def _():
            for c in range(acc.shape[1] // HEAD_DIM):
                a = acc[:, c * HEAD_DIM:(c + 1) * HEAD_DIM]
                y = a * lax.rsqrt(jnp.mean(a * a, axis=-1, keepdims=True) + RMS_EPS) * g_ref[...]
                o_ref[:, c * HEAD_DIM:(c + 1) * HEAD_DIM] = y.astype(o_ref.dtype)

        @pl.when(j >= n_norm_tiles)
        def _():
            o_ref[...] = acc.astype(o_ref.dtype)


def matmul(x, w, n_out, out_dtype, tm, tn, mode="plain", resid=None, gain=None, n_norm_tiles=0):
    m, k = x.shape
    in_specs = [pl.BlockSpec((tm, k), lambda j, i: (i, 0)), pl.BlockSpec((k, tn), lambda j, i: (0, j))]
    args = [x, w]
    if mode == "resid":
        in_specs.append(pl.BlockSpec((tm, tn), lambda j, i: (i, j)))
        args.append(resid)
    elif mode == "headnorm":
        in_specs.append(pl.BlockSpec((1, HEAD_DIM), lambda j, i: (0, 0)))
        args.append(gain.reshape(1, HEAD_DIM))
    return pl.pallas_call(
        functools.partial(_mm_body, mode=mode, n_norm_tiles=n_norm_tiles),
        grid=(n_out // tn, m // tm),
        in_specs=in_specs,
        out_specs=pl.BlockSpec((tm, tn), lambda j, i: (i, j)),
        out_shape=jax.ShapeDtypeStruct((m, n_out), out_dtype),
        scratch_shapes=[pltpu.VMEM((k, tn), BF16)],
        compiler_params=_cparams("arbitrary", "arbitrary"), name="matmul_" + mode)(*args)


def _gates_body(x_ref, w_ref, b_ref, o_ref):
    pre = _dot(x_ref[...], w_ref[...].astype(BF16)) + b_ref[...]
    lane = lax.broadcasted_iota(jnp.int32, pre.shape, 1)
    o_ref[...] = jnp.where(lane < A_HEADS, pre, jnp.where(lane < 2 * A_HEADS, _log_sigmoid(pre), 0.0))


def mlstm_gates(x, w_gate, b_gate, tm):
    m, k = x.shape
    w = jnp.pad(w_gate, ((0, 0), (0, LANES - w_gate.shape[1])))
    b = jnp.pad(b_gate.astype(F32), (0, LANES - b_gate.shape[0])).reshape(1, LANES)
    return pl.pallas_call(
        _gates_body, grid=(m // tm,),
        in_specs=[pl.BlockSpec((tm, k), lambda i: (i, 0)), pl.BlockSpec((k, LANES), lambda i: (0, 0)),
                  pl.BlockSpec((1, LANES), lambda i: (0, 0))],
        out_specs=pl.BlockSpec((tm, LANES), lambda i: (i, 0)),
        out_shape=jax.ShapeDtypeStruct((m, LANES), F32),
        compiler_params=_cparams("arbitrary"), name="mlstm_gates")(x, w, b)


def _mlstm_body(*refs, chunk, lb, t_valid, t_total, has_init):
    if has_init:
        q_ref, k_ref, v_ref, o_ref, ga_ref, gh_ref, c0_ref, n0_ref, m0_ref, hg_ref, c_ref, n_ref, m_ref = refs
    else:
        q_ref, k_ref, v_ref, o_ref, ga_ref, gh_ref, hg_ref, c_ref, n_ref, m_ref = refs
    h = pl.program_id(1)
    c = pl.program_id(2)

    @pl.when(c == 0)
    def _():
        if has_init:
            c_ref[...] = c0_ref[...]
            n_ref[...] = n0_ref[...]
            m_ref[...] = m0_ref[...]
        else:
            c_ref[...] = jnp.zeros_like(c_ref)
            n_ref[...] = jnp.zeros_like(n_ref)
            m_ref[...] = jnp.zeros_like(m_ref)

    def load(ref):
        x = ref[...]
        if lb == chunk:
            return x
        return jnp.concatenate([x, jnp.zeros((chunk - lb, x.shape[1]), x.dtype)], axis=0)

    q = load(q_ref)
    k = load(k_ref)
    v = load(v_ref)
    ga = load(ga_ref)
    c_state = c_ref[0, 0]
    n_state = n_ref[0, 0]
    m_state = m_ref[0, 0][:, 0:1]

    lane = lax.broadcasted_iota(jnp.int32, (chunk, LANES), 1)
    i_col = jnp.sum(jnp.where(lane == h, ga, 0.0), axis=1, keepdims=True)
    f_col = jnp.sum(jnp.where(lane == h + A_HEADS, ga, 0.0), axis=1, keepdims=True)
    if t_valid < t_total:
        t_idx = lax.broadcasted_iota(jnp.int32, (chunk, 1), 0) + c * chunk
        i_col = jnp.where(t_idx < t_valid, i_col, NEG_BIG)
        f_col = jnp.where(t_idx < t_valid, f_col, 0.0)

    row_i = lax.broadcasted_iota(jnp.int32, (chunk, chunk), 0)
    col_i = lax.broadcasted_iota(jnp.int32, (chunk, chunk), 1)
    causal = col_i <= row_i
    g2 = jnp.where(lane == 0, i_col, jnp.where(lane == 1, f_col, 0.0))
    cum = _dot(causal.astype(F32), g2, precision=HIGHEST)
    g3 = jnp.where(lane == 1, cum, g2)
    g3t = g3.T
    i_row = g3t[0:1, :]
    b_row = g3t[1:2, :]
    b_col = g3[:, 1:2]

    dmat = jnp.where(causal, b_col - b_row + i_row, NEG_BIG)
    inter = b_col + m_state
    m_t = jnp.maximum(inter, jnp.max(dmat, axis=1, keepdims=True))
    w_intra = jnp.exp(dmat - m_t)
    w_inter = jnp.exp(inter - m_t)
    scale = A_QK_DIM ** -0.5
    s = _dot_nt(q, k) * scale * w_intra
    num = w_inter * (_dot(q, c_state.astype(BF16)) * scale) + _dot(s.astype(BF16), v)
    qn = jnp.sum(q.astype(F32) * n_state, axis=1, keepdims=True) * scale
    den = w_inter * qn + jnp.sum(s, axis=1, keepdims=True)
    hh = num / jnp.maximum(jnp.abs(den), jnp.exp(-m_t))

    hn = hh * lax.rsqrt(jnp.mean(hh * hh, axis=1, keepdims=True) + RMS_EPS) * gh_ref[0]
    og = o_ref[...].astype(F32)
    out = hn[0:lb] * (1.0 / (1.0 + jnp.exp(-og)))
    hg_ref[...] = out.astype(hg_ref.dtype)

    b_last = b_col[chunk - 1:chunk, :]
    g = b_last - b_col + i_col
    m_new = jnp.maximum(b_last + m_state, jnp.max(g, axis=0, keepdims=True))
    wk = jnp.exp(g - m_new)
    decay = jnp.exp(b_last + m_state - m_new)
    kw = k.astype(F32) * wk
    c_ref[0, 0] = decay * c_state + _dot(kw.T.astype(BF16), v)
    n_ref[0, 0] = decay * n_state + jnp.sum(kw, axis=0, keepdims=True)
    m_ref[0, 0] = jnp.broadcast_to(m_new, (1, LANES))


def mlstm(p_main, gates, g_hnorm, hg_prev, row0, n_seq, t_total, t_valid, lb, chunk, init=None):
    m = p_main.shape[0]
    nc = t_total // lb
    rb0 = row0 // lb

    def rows(b, h, c):
        return rb0 + b * nc + c

    qk_blocks = A_QK // A_QK_DIM
    in_specs = [
        pl.BlockSpec((lb, A_QK_DIM), lambda b, h, c: (rows(b, h, c), h)),
        pl.BlockSpec((lb, A_QK_DIM), lambda b, h, c: (rows(b, h, c), qk_blocks + h)),
        pl.BlockSpec((lb, A_V_DIM), lambda b, h, c: (rows(b, h, c), 2 * A_QK // A_V_DIM + h)),
        pl.BlockSpec((lb, A_V_DIM), lambda b, h, c: (rows(b, h, c), (2 * A_QK + A_VD) // A_V_DIM + h)),
        pl.BlockSpec((lb, LANES), lambda b, h, c: (rows(b, h, c), 0)),
        pl.BlockSpec((1, 1, A_V_DIM), lambda b, h, c: (h, 0, 0)),
    ]
    args = [p_main, p_main, p_main, p_main, gates, g_hnorm.reshape(A_HEADS, 1, A_V_DIM)]
    state_specs = [
        pl.BlockSpec((1, 1, A_QK_DIM, A_V_DIM), lambda b, h, c: (b, h, 0, 0)),
        pl.BlockSpec((1, 1, 1, A_QK_DIM), lambda b, h, c: (b, h, 0, 0)),
        pl.BlockSpec((1, 1, 1, LANES), lambda b, h, c: (b, h, 0, 0)),
    ]
    has_init = init is not None
    if has_init:
        c0, n0, m0 = init
        in_specs += state_specs
        args += [c0.astype(F32), n0.astype(F32).reshape(n_seq, A_HEADS, 1, A_QK_DIM),
                 jnp.broadcast_to(m0.astype(F32)[:, :, None, None], (n_seq, A_HEADS, 1, LANES))]
    aliases = {}
    if hg_prev is not None:
        in_specs.append(pl.BlockSpec(memory_space=pl.ANY))
        args.append(hg_prev)
        aliases = {len(args) - 1: 0}

    def body(*refs):
        if hg_prev is not None:
            n_in = len(args)
            refs = refs[:n_in - 1] + refs[n_in:]
        _mlstm_body(*refs, chunk=chunk, lb=lb, t_valid=t_valid, t_total=t_total, has_init=has_init)

    hg, c_out, n_out, m_out = pl.pallas_call(
        body, grid=(n_seq, A_HEADS, nc),
        in_specs=in_specs,
        out_specs=[pl.BlockSpec((lb, A_V_DIM), lambda b, h, c: (rows(b, h, c), h))] + state_specs,
        out_shape=[jax.ShapeDtypeStruct((m, A_VD), BF16),
                   jax.ShapeDtypeStruct((n_seq, A_HEADS, A_QK_DIM, A_V_DIM), F32),
                   jax.ShapeDtypeStruct((n_seq, A_HEADS, 1, A_QK_DIM), F32),
                   jax.ShapeDtypeStruct((n_seq, A_HEADS, 1, LANES), F32)],
        input_output_aliases=aliases,
        compiler_params=_cparams("arbitrary", "arbitrary", "arbitrary"), name="mlstm")(*args)
    return hg, c_out, n_out.reshape(n_seq, A_HEADS, A_QK_DIM), m_out[:, :, 0, 0]


def _sb_block(z, tri, carry, mask):
    l1p = jnp.log1p(jnp.exp(-jnp.abs(z)))
    ls = jnp.minimum(z, 0.0) - l1p
    u = ls - z
    if mask is not None:
        u = jnp.where(mask, u, 0.0)
    u_hi = u.astype(BF16)
    u_lo = (u - u_hi.astype(F32)).astype(BF16)
    within = _dot(u_hi, tri) + _dot(u_lo, tri)
    a = jnp.exp(ls + within + carry)
    if mask is not None:
        a = jnp.where(mask, a, 0.0)
    return a, carry + within[:, 0:1] + u[:, 0:1]


def _strict_tri(tk):
    r = lax.broadcasted_iota(jnp.int32, (tk, tk), 0)
    c = lax.broadcasted_iota(jnp.int32, (tk, tk), 1)
    return (r > c).astype(BF16)


def _attn_prompt_body(q_ref, k_ref, v_ref, b_ref, o_ref, *, tq, tk):
    qi = pl.program_id(2)
    scale = HEAD_DIM ** -0.5
    qs = jnp.concatenate([q_ref[:, g * HEAD_DIM:(g + 1) * HEAD_DIM] for g in range(Q_PER_KV)], axis=0)
    rows = Q_PER_KV * tq
    bias = jnp.concatenate([jnp.broadcast_to(b_ref[0, :, g:g + 1], (tq, 1)) for g in range(Q_PER_KV)], axis=0)
    tri = _strict_tri(tk)
    t_loc = lax.broadcasted_iota(jnp.int32, (tq, tk), 0)
    s_loc = lax.broadcasted_iota(jnp.int32, (tq, tk), 1)
    nkb = tq // tk

    def visit(j, carry_acc, diag_off):
        carry, acc = carry_acc
        start = pl.multiple_of(j * tk, tk)
        kb = k_ref[pl.ds(start, tk), :].astype(BF16)
        vb = v_ref[pl.ds(start, tk), :].astype(BF16)
        z = _dot_nt(qs, kb) * scale + bias
        mask = None
        if diag_off is not None:
            m1 = (s_loc + diag_off * tk) < t_loc
            mask = jnp.concatenate([m1] * Q_PER_KV, axis=0)
        a, carry = _sb_block(z, tri, carry, mask)
        return carry, acc + _dot(a.astype(BF16), vb)

    state = (jnp.zeros((rows, 1), F32), jnp.zeros((rows, HEAD_DIM), F32))
    for d in reversed(range(nkb)):
        state = visit(qi * nkb + d, state, d)
    n_before = qi * nkb
    state = lax.fori_loop(0, n_before, lambda it, st: visit(n_before - 1 - it, st, None), state)
    acc = state[1]
    for g in range(Q_PER_KV):
        o_ref[:, g * HEAD_DIM:(g + 1) * HEAD_DIM] = acc[g * tq:(g + 1) * tq].astype(o_ref.dtype)


def attn_prompt(q_all, kv_all, b_logit, tq=128, tk=128):
    m = q_all.shape[0]
    nq = SEQ // tq
    gw = Q_PER_KV * HEAD_DIM
    bias = b_logit.astype(F32).reshape(KV_HEADS, 1, Q_PER_KV)
    return pl.pallas_call(
        functools.partial(_attn_prompt_body, tq=tq, tk=tk),
        grid=(BATCH, KV_HEADS, nq),
        in_specs=[pl.BlockSpec((tq, gw), lambda b, h, i: (b * nq + i, h)),
                  pl.BlockSpec((SEQ, HEAD_DIM), lambda b, h, i: (b, h)),
                  pl.BlockSpec((SEQ, HEAD_DIM), lambda b, h, i: (b, KV_HEADS + h)),
                  pl.BlockSpec((1, 1, Q_PER_KV), lambda b, h, i: (h, 0, 0))],
        out_specs=pl.BlockSpec((tq, gw), lambda b, h, i: (b * nq + i, h)),
        out_shape=jax.ShapeDtypeStruct((m, B_HEADS * HEAD_DIM), BF16),
        compiler_params=_cparams("arbitrary", "arbitrary", "arbitrary"), name="attn_prompt")(q_all, kv_all, kv_all, bias)


QROWS = 8


def _attn_sample_body(pt_ref, q_ref, kn_ref, vn_ref, ck_ref, cv_ref, b_ref, oin_ref, o_ref, qs_ref, car_ref, acc_ref,
                      *, n_pages):
    del pt_ref, oin_ref
    p = pl.program_id(1)
    scale = HEAD_DIM ** -0.5
    rows_h = Q_PER_KV * QROWS
    tri = _strict_tri(PAGE_SIZE)

    def process(get_k, get_v, mask):
        for kh in range(KV_HEADS):
            r0 = kh * rows_h
            qh = qs_ref[r0:r0 + rows_h, :].astype(BF16)
            z = _dot_nt(qh, get_k(kh).astype(BF16)) * scale + b_ref[r0:r0 + rows_h, :]
            a, carry = _sb_block(z, tri, car_ref[r0:r0 + rows_h, :], mask)
            car_ref[r0:r0 + rows_h, :] = carry
            acc_ref[r0:r0 + rows_h, :] += _dot(a.astype(BF16), get_v(kh).astype(BF16))

    @pl.when(p == 0)
    def _():
        for hq in range(B_HEADS):
            qs_ref[hq * QROWS:(hq + 1) * QROWS, :] = q_ref[:, hq * HEAD_DIM:(hq + 1) * HEAD_DIM].astype(F32)[0:QROWS]
        car_ref[...] = jnp.zeros_like(car_ref)
        acc_ref[...] = jnp.zeros_like(acc_ref)
        pad = jnp.zeros((PAGE_SIZE - S_PAD, HEAD_DIM), F32)
        t_loc = lax.broadcasted_iota(jnp.int32, (rows_h, PAGE_SIZE), 0) % QROWS
        s_loc = lax.broadcasted_iota(jnp.int32, (rows_h, PAGE_SIZE), 1)
        process(lambda kh: jnp.concatenate([kn_ref[:, kh * HEAD_DIM:(kh + 1) * HEAD_DIM], pad], axis=0),
                lambda kh: jnp.concatenate([vn_ref[:, kh * HEAD_DIM:(kh + 1) * HEAD_DIM], pad], axis=0),
                s_loc < t_loc)

    process(lambda kh: ck_ref[0, :, kh, :], lambda kh: cv_ref[0, :, kh, :], None)

    @pl.when(p == n_pages - 1)
    def _():
        pad = jnp.zeros((S_PAD - QROWS, HEAD_DIM), F32)
        for hq in range(B_HEADS):
            full = jnp.concatenate([acc_ref[hq * QROWS:(hq + 1) * QROWS, :], pad], axis=0)
            o_ref[:, hq * HEAD_DIM:(hq + 1) * HEAD_DIM] = full.astype(o_ref.dtype)


def attn_sample(q_all, kv_all, cache_k, cache_v, page_table, b_logit, o_prev):
    n_seq, n_pages = page_table.shape
    rb0 = MP // S_PAD
    kvw = KV_HEADS * HEAD_DIM
    bias = jnp.broadcast_to(b_logit.astype(F32)[:, None, None], (B_HEADS, QROWS, 1)).reshape(B_HEADS * QROWS, 1)
    page = lambda s, p, pt: (pt[s * n_pages + n_pages - 1 - p], 0, 0, 0)
    grid_spec = pltpu.PrefetchScalarGridSpec(
        num_scalar_prefetch=1, grid=(n_seq, n_pages),
        in_specs=[pl.BlockSpec((S_PAD, B_HEADS * HEAD_DIM), lambda s, p, pt: (rb0 + s, 0)),
                  pl.BlockSpec((S_PAD, kvw), lambda s, p, pt: (rb0 + s, 0)),
                  pl.BlockSpec((S_PAD, kvw), lambda s, p, pt: (rb0 + s, 1)),
                  pl.BlockSpec((1, PAGE_SIZE, KV_HEADS, HEAD_DIM), page),
                  pl.BlockSpec((1, PAGE_SIZE, KV_HEADS, HEAD_DIM), page),
                  pl.BlockSpec((B_HEADS * QROWS, 1), lambda s, p, pt: (0, 0)),
                  pl.BlockSpec(memory_space=pl.ANY)],
        out_specs=pl.BlockSpec((S_PAD, B_HEADS * HEAD_DIM), lambda s, p, pt: (rb0 + s, 0)),
        scratch_shapes=[pltpu.VMEM((B_HEADS * QROWS, HEAD_DIM), F32),
                        pltpu.VMEM((B_HEADS * QROWS, 1), F32),
                        pltpu.VMEM((B_HEADS * QROWS, HEAD_DIM), F32)])
    return pl.pallas_call(
        functools.partial(_attn_sample_body, n_pages=n_pages),
        grid_spec=grid_spec,
        out_shape=jax.ShapeDtypeStruct(o_prev.shape, o_prev.dtype),
        input_output_aliases={7: 0},
        compiler_params=_cparams("arbitrary", "arbitrary"), name="attn_sample")(
            page_table.reshape(-1), q_all, kv_all, kv_all, cache_k, cache_v, bias, o_prev)


def _router_body(h_ref, g_ref, w_ref, xn_ref, idx_ref, gate_ref):
    x = h_ref[...]
    xn = x * lax.rsqrt(jnp.mean(x * x, axis=-1, keepdims=True) + RMS_EPS) * g_ref[...]
    xn_ref[...] = xn
    logits = _dot(xn, w_ref[...], precision=HIGHEST)
    lane = lax.broadcasted_iota(jnp.int32, logits.shape, 1)
    lane_f = lane.astype(F32)
    ninf = -jnp.inf

    def first_max(vals):
        mx = jnp.max(vals, axis=1, keepdims=True)
        return mx, jnp.min(jnp.where(vals == mx, lane_f, float(LANES)), axis=1, keepdims=True)

    gl = jnp.where(lane < N_GROUPS, logits, ninf)
    gmax, grp = first_max(gl)
    g_gate = 1.0 / jnp.sum(jnp.exp(gl - gmax), axis=1, keepdims=True)
    e_lane = lane - N_GROUPS
    lane_grp = jnp.right_shift(e_lane, 3).astype(F32)
    in_grp = (e_lane >= 0) & (e_lane < N_EXPERTS) & (lane_grp == grp)
    el = jnp.where(in_grp, logits, ninf)
    t1, i1 = first_max(el)
    t2, i2 = first_max(jnp.where(lane_f == i1, ninf, el))
    e21 = jnp.exp(t2 - t1)
    g1 = g_gate / (1.0 + e21)
    g2 = g_gate * e21 / (1.0 + e21)
    idx_ref[...] = jnp.where(lane == 0, i1, jnp.where(lane == 1, i2, float(N_GROUPS))).astype(jnp.int32) - N_GROUPS
    gate_ref[...] = jnp.where(lane == 0, g1, jnp.where(lane == 1, g2, 0.0))


def router(h, g_norm, w_group, w_router, tr):
    m, d = h.shape
    w = jnp.pad(jnp.concatenate([w_group, w_router], axis=1).astype(F32), ((0, 0), (0, LANES - N_GROUPS - N_EXPERTS)))
    row = pl.BlockSpec((tr, d), lambda i: (i, 0))
    small = pl.BlockSpec((tr, LANES), lambda i: (i, 0))
    xn, idx, gate = pl.pallas_call(
        _router_body, grid=(m // tr,),
        in_specs=[row, pl.BlockSpec((1, d), lambda i: (0, 0)), pl.BlockSpec((d, LANES), lambda i: (0, 0))],
        out_specs=[row, small, small],
        out_shape=[jax.ShapeDtypeStruct((m, d), F32), jax.ShapeDtypeStruct((m, LANES), jnp.int32),
                   jax.ShapeDtypeStruct((m, LANES), F32)],
        compiler_params=_cparams("arbitrary"), name="router")(h, g_norm.reshape(1, d), w)
    return xn, idx[:, :2], gate[:, :2]


BM = 512
SUB = 256
UNSORT_CHUNK = 256


def _moe_plan(idx, gates):
    m = idx.shape[0]
    flat_e = idx.reshape(-1)
    onehot = (flat_e[:, None] == jnp.arange(N_EXPERTS, dtype=jnp.int32)[None, :]).astype(jnp.int32)
    csum = jnp.cumsum(onehot, axis=0)
    rank = jnp.take_along_axis(csum, flat_e[:, None], axis=1)[:, 0] - 1
    counts = csum[-1]
    nblk = (counts + BM - 1) // BM
    ends = jnp.cumsum(nblk)
    bstart = ends - nblk
    dest = (bstart[flat_e] * BM + rank).astype(jnp.int32)
    n_blocks = N_EXPERTS + (2 * m) // BM
    n_used = ends[-1]
    blk = jnp.minimum(jnp.arange(n_blocks, dtype=jnp.int32), n_used - 1)
    be = jnp.minimum(jnp.searchsorted(ends, blk, side="right"), N_EXPERTS - 1).astype(jnp.int32)
    cnt = jnp.clip(counts[be] - (blk - bstart[be]) * BM, 0, BM)
    cnt = jnp.where(jnp.arange(n_blocks) < n_used, cnt, 0).astype(jnp.int32)
    n_rows = n_blocks * BM
    tok = jnp.arange(2 * m, dtype=jnp.int32) // 2
    row_src = jnp.zeros((n_rows,), jnp.int32).at[dest].set(tok)
    row_gate = jnp.zeros((n_rows,), F32).at[dest].set(gates.reshape(-1))
    dest_by_slot = dest.reshape(m, 2).T.reshape(-1)
    gather_lim = ((cnt + SUB - 1) // SUB) * SUB
    return dict(be=be, bx=blk.astype(jnp.int32), cnt=cnt, row_src=row_src, row_gate=row_gate,
                dest_by_slot=dest_by_slot, gather_lim=gather_lim, n_blocks=n_blocks, n_rows=n_rows)


def _copy_rows_body(lim_ref, idx_ref, src_ref, dst_ref, sem, *, chunk):
    i = pl.program_id(0)
    base = i * chunk
    n = lim_ref[i]

    def desc(r):
        return pltpu.make_async_copy(src_ref.at[pl.ds(idx_ref[0, 0, r], 1)], dst_ref.at[pl.ds(base + r, 1)], sem)

    def start(r, carry):
        desc(r).start()
        return carry

    def wait(r, carry):
        desc(r).wait()
        return carry

    lax.fori_loop(0, n, start, 0)
    lax.fori_loop(0, n, wait, 0)


def copy_rows(src, row_idx, limits, chunk):
    n = row_idx.shape[0]
    d = src.shape[1]
    return pl.pallas_call(
        functools.partial(_copy_rows_body, chunk=chunk),
        grid_spec=pltpu.PrefetchScalarGridSpec(
            num_scalar_prefetch=1, grid=(n // chunk,),
            in_specs=[pl.BlockSpec((1, 1, chunk), lambda i, lim: (i, 0, 0), memory_space=pltpu.SMEM),
                      pl.BlockSpec(memory_space=pl.ANY)],
            out_specs=pl.BlockSpec(memory_space=pl.ANY),
            scratch_shapes=[pltpu.SemaphoreType.DMA(())]),
        out_shape=jax.ShapeDtypeStruct((n, d), src.dtype),
        compiler_params=_cparams("arbitrary"), name="copy_rows")(limits, row_idx.reshape(n // chunk, 1, chunk), src)


def _ffn_in_body(be_ref, bx_ref, cnt_ref, x_ref, wg_ref, wu_ref, o_ref, wgb_ref, wub_ref):
    del bx_ref
    b = pl.program_id(1)
    cnt = cnt_ref[b]
    new_w = (b == 0) | (be_ref[b] != be_ref[jnp.maximum(b - 1, 0)])

    @pl.when(new_w)
    def _():
        wgb_ref[...] = wg_ref[0, 0].astype(BF16)
        wub_ref[...] = wu_ref[0, 0].astype(BF16)

    for sub in range(BM // SUB):
        @pl.when(sub * SUB < cnt)
        def _():
            x = x_ref[sub * SUB:(sub + 1) * SUB, :].astype(BF16)
            g = _dot(x, wgb_ref[...])
            u = _dot(x, wub_ref[...])
            o_ref[sub * SUB:(sub + 1) * SUB, :] = (g * (1.0 / (1.0 + jnp.exp(-g))) * u).astype(o_ref.dtype)


def _ffn_out_body(be_ref, bx_ref, cnt_ref, h_ref, w_ref, gate_ref, o_ref, wb_ref):
    del bx_ref
    b = pl.program_id(1)
    cnt = cnt_ref[b]
    new_w = (b == 0) | (be_ref[b] != be_ref[jnp.maximum(b - 1, 0)])

    @pl.when(new_w)
    def _():
        wb_ref[...] = w_ref[0, 0].astype(BF16)

    for sub in range(BM // SUB):
        @pl.when(sub * SUB < cnt)
        def _():
            y = _dot(h_ref[sub * SUB:(sub + 1) * SUB, :], wb_ref[...])
            o_ref[sub * SUB:(sub + 1) * SUB, :] = y * gate_ref[sub * SUB:(sub + 1) * SUB, 0:1]


def expert_ffn(x_sorted, plan, w_in, w_out, layer, tn_in=256, tn_out=1024):
    n_rows, d = x_sorted.shape
    nb = plan["n_blocks"]
    nci = D_EXPERT // tn_in
    h_mid = pl.pallas_call(
        _ffn_in_body,
        grid_spec=pltpu.PrefetchScalarGridSpec(
            num_scalar_prefetch=3, grid=(nci, nb),
            in_specs=[pl.BlockSpec((BM, d), lambda c, b, be, bx, cnt: (bx[b], 0)),
                      pl.BlockSpec((1, 1, d, tn_in), lambda c, b, be, bx, cnt: (layer, be[b], 0, c)),
                      pl.BlockSpec((1, 1, d, tn_in), lambda c, b, be, bx, cnt: (layer, be[b], 0, nci + c))],
            out_specs=pl.BlockSpec((BM, tn_in), lambda c, b, be, bx, cnt: (bx[b], c)),
            scratch_shapes=[pltpu.VMEM((d, tn_in), BF16), pltpu.VMEM((d, tn_in), BF16)]),
        out_shape=jax.ShapeDtypeStruct((n_rows, D_EXPERT), BF16),
        compiler_params=_cparams("arbitrary", "arbitrary"), name="ffn_in")(
            plan["be"], plan["bx"], plan["cnt"], x_sorted, w_in, w_in)
    gate = jnp.broadcast_to(plan["row_gate"][:, None], (n_rows, LANES))
    nco = d // tn_out
    return pl.pallas_call(
        _ffn_out_body,
        grid_spec=pltpu.PrefetchScalarGridSpec(
            num_scalar_prefetch=3, grid=(nco, nb),
            in_specs=[pl.BlockSpec((BM, D_EXPERT), lambda c, b, be, bx, cnt: (bx[b], 0)),
                      pl.BlockSpec((1, 1, D_EXPERT, tn_out), lambda c, b, be, bx, cnt: (layer, be[b], 0, c)),
                      pl.BlockSpec((BM, LANES), lambda c, b, be, bx, cnt: (bx[b], 0))],
            out_specs=pl.BlockSpec((BM, tn_out), lambda c, b, be, bx, cnt: (bx[b], c)),
            scratch_shapes=[pltpu.VMEM((D_EXPERT, tn_out), BF16)]),
        out_shape=jax.ShapeDtypeStruct((n_rows, d), F32),
        compiler_params=_cparams("arbitrary", "arbitrary"), name="ffn_out")(
            plan["be"], plan["bx"], plan["cnt"], h_mid, w_out, gate)


def _combine_body(h_ref, y_ref, *rest, n_norm):
    hn = h_ref[...] + y_ref[0] + y_ref[1]
    if n_norm == 0:
        (o_ref,) = rest
        o_ref[...] = hn
        return
    g_ref, o_ref, *norm_refs = rest
    o_ref[...] = hn
    xr = hn * lax.rsqrt(jnp.mean(hn * hn, axis=-1, keepdims=True) + RMS_EPS)
    for j, n_ref in enumerate(norm_refs):
        n_ref[...] = (xr * g_ref[j:j + 1, :]).astype(n_ref.dtype)


def combine(h, y2, gains, tr):
    m, d = h.shape
    n_norm = 0 if gains is None else gains.shape[0]
    row = pl.BlockSpec((tr, d), lambda i: (i, 0))
    in_specs = [row, pl.BlockSpec((2, tr, d), lambda i: (0, i, 0))]
    args = [h, y2.reshape(2, m, d)]
    if n_norm:
        in_specs.append(pl.BlockSpec((n_norm, d), lambda i: (0, 0)))
        args.append(gains)
    return pl.pallas_call(
        functools.partial(_combine_body, n_norm=n_norm), grid=(m // tr,),
        in_specs=in_specs, out_specs=[row] * (1 + n_norm),
        out_shape=[jax.ShapeDtypeStruct((m, d), F32)] + [jax.ShapeDtypeStruct((m, d), BF16)] * n_norm,
        compiler_params=_cparams("arbitrary"), name="combine")(*args)


def hier_moe_layer(h, g_norm, w_group, w_router, w_in, w_out, layer, next_gains, tr):
    xn, idx, gates = router(h, g_norm, w_group, w_router, tr)
    plan = _moe_plan(idx, gates)
    x_sorted = copy_rows(xn, plan["row_src"], plan["gather_lim"], BM)
    y_sorted = expert_ffn(x_sorted, plan, w_in, w_out, layer)
    n_back = plan["dest_by_slot"].shape[0]
    y2 = copy_rows(y_sorted, plan["dest_by_slot"], jnp.full((n_back // UNSORT_CHUNK,), UNSORT_CHUNK, jnp.int32),
                   UNSORT_CHUNK)
    return combine(h, y2, next_gains, tr)


TM = 640
TR = 128


def kernel(x_prompt, x_sample, state_C, state_n, state_m, cache_k, cache_v, page_table, norm_mix, w_in_a, b_gate_a,
           g_hnorm_a, w_out_a, g_kv, w_kv, g_knorm, w_q_b, g_qnorm_b, b_logit_b, w_out_b, norm_ffn, w_group,
           w_router, w_moe_in, w_moe_out):
    xs = jnp.pad(x_sample, ((0, 0), (0, S_PAD - DEC_SEQ), (0, 0))).reshape(MS, D_MODEL)
    h0 = jnp.concatenate([x_prompt.reshape(MP, D_MODEL), xs], axis=0)

    (xn0,) = rms_norm(h0, norm_mix[0:1], [BF16], TR)
    p_main = matmul(xn0, w_in_a[0], A_MAIN, BF16, TM, 512)
    gates = mlstm_gates(xn0, w_in_a[0][:, A_MAIN:], b_gate_a[0], TM)
    hg, pc, pn, pm = mlstm(p_main, gates, g_hnorm_a[0], None, 0, BATCH, SEQ, SEQ, 256, 256)
    hg, sc, sn, sm = mlstm(p_main, gates, g_hnorm_a[0], hg, MP, DEC_BATCH, S_PAD, DEC_SEQ, S_PAD, LANES,
                           init=(state_C[0], state_n[0], state_m[0]))
    h1 = matmul(hg, w_out_a[0], D_MODEL, F32, TM, 512, mode="resid", resid=h0)
    h1, xn1, xkv = hier_moe_layer(h1, norm_ffn[0], w_group[0], w_router[0], w_moe_in, w_moe_out, 0,
                                  jnp.stack([norm_mix[1], g_kv]), TR)

    kv_all = matmul(xkv, w_kv, 2 * KV_HEADS * HEAD_DIM, F32, TM, 512, mode="headnorm", gain=g_knorm,
                    n_norm_tiles=KV_HEADS * HEAD_DIM // 512)
    q_all = matmul(xn1, w_q_b[0], B_HEADS * HEAD_DIM, BF16, TM, 512, mode="headnorm", gain=g_qnorm_b[0],
                   n_norm_tiles=B_HEADS * HEAD_DIM // 512)
    o_all = attn_prompt(q_all, kv_all, b_logit_b[0])
    o_all = attn_sample(q_all, kv_all, cache_k, cache_v, page_table, b_logit_b[0], o_all)
    h2 = matmul(o_all, w_out_b[0], D_MODEL, F32, TM, 512, mode="resid", resid=h1)
    (h3,) = hier_moe_layer(h2, norm_ffn[1], w_group[1], w_router[1], w_moe_in, w_moe_out, 1, None, TR)

    def sample_rows(a):
        return a[MP:].reshape((DEC_BATCH, S_PAD) + a.shape[1:])[:, :DEC_SEQ]

    kvw = KV_HEADS * HEAD_DIM
    y_prompt = h3[:MP].reshape(BATCH, SEQ, D_MODEL)
    y_sample = sample_rows(h3)
    prompt_k = kv_all[:MP, :kvw].reshape(BATCH, SEQ, KV_HEADS, HEAD_DIM)
    prompt_v = kv_all[:MP, kvw:].reshape(BATCH, SEQ, KV_HEADS, HEAD_DIM)
    sample_k = sample_rows(kv_all[:, :kvw]).reshape(DEC_BATCH, DEC_SEQ, KV_HEADS, HEAD_DIM)
    sample_v = sample_rows(kv_all[:, kvw:]).reshape(DEC_BATCH, DEC_SEQ, KV_HEADS, HEAD_DIM)
    return (y_prompt, y_sample, pc[None], pn[None], pm[None], prompt_k, prompt_v,
            sc[None], sn[None], sm[None], sample_k, sample_v)
```

```python
import functools

import jax
import jax.numpy as jnp
from jax import lax
from jax.experimental import pallas as pl
from jax.experimental.pallas import tpu as pltpu

F32 = jnp.float32
BF16 = jnp.bfloat16

D_MODEL = 4096
BATCH = 4
SEQ = 2048
DEC_BATCH = 8
DEC_SEQ = 4
PAGE_SIZE = 128
A_HEADS = 8
A_QK_DIM = 256
A_V_DIM = 512
A_QK = A_HEADS * A_QK_DIM
A_VD = A_HEADS * A_V_DIM
A_MAIN = 2 * A_QK + A_VD + D_MODEL
HEAD_DIM = 128
B_HEADS = 32
KV_HEADS = 8
Q_PER_KV = B_HEADS // KV_HEADS
N_GROUPS = 4
EXPERTS_PER_GROUP = 8
N_EXPERTS = N_GROUPS * EXPERTS_PER_GROUP
D_EXPERT = 1024
RMS_EPS = 1e-6

LANES = 128
MP = BATCH * SEQ
S_PAD = 16
MS = DEC_BATCH * S_PAD
M_ALL = MP + MS
NEG_BIG = -1e30
VMEM_LIMIT = 56 * 1024 * 1024

HIGHEST = lax.Precision.HIGHEST


def _cparams(*sem):
    return pltpu.CompilerParams(dimension_semantics=sem, vmem_limit_bytes=VMEM_LIMIT)


def _dot(a, b, precision=None):
    return jnp.dot(a, b, preferred_element_type=F32, precision=precision)


def _dot_nt(a, b):
    return lax.dot_general(a, b, (((1,), (1,)), ((), ())), preferred_element_type=F32)


def _log_sigmoid(x):
    return jnp.minimum(x, 0.0) - jnp.log1p(jnp.exp(-jnp.abs(x)))


def _rms_body(x_ref, g_ref, *out_refs):
    x = x_ref[...].astype(F32)
    xr = x * lax.rsqrt(jnp.mean(x * x, axis=-1, keepdims=True) + RMS_EPS)
    for j, o_ref in enumerate(out_refs):
        o_ref[...] = (xr * g_ref[j:j + 1, :]).astype(o_ref.dtype)


def rms_norm(x, gains, out_dtypes, tr):
    m, d = x.shape
    n = len(out_dtypes)
    row = pl.BlockSpec((tr, d), lambda i: (i, 0))
    return pl.pallas_call(
        _rms_body, grid=(m // tr,),
        in_specs=[row, pl.BlockSpec((n, d), lambda i: (0, 0))],
        out_specs=[row] * n,
        out_shape=[jax.ShapeDtypeStruct((m, d), dt) for dt in out_dtypes],
        compiler_params=_cparams("arbitrary"), name="rms_norm")(x, gains)


def _mm_body(x_ref, w_ref, *rest, mode, n_norm_tiles):
    if mode == "resid":
        r_ref, o_ref, wb_ref = rest
    elif mode == "headnorm":
        g_ref, o_ref, wb_ref = rest
    else:
        o_ref, wb_ref = rest
    j = pl.program_id(0)

    @pl.when(pl.program_id(1) == 0)
    def _():
        wb_ref[...] = w_ref[...].astype(BF16)

    acc = _dot(x_ref[...], wb_ref[...])
    if mode == "plain":
        o_ref[...] = acc.astype(o_ref.dtype)
    elif mode == "resid":
        o_ref[...] = r_ref[...] + acc
    else:
        @pl.when(j < n_norm_tiles)
        def _():
            for c in range(acc.shape[1] // HEAD_DIM):
                a = acc[:, c * HEAD_DIM:(c + 1) * HEAD_DIM]
                y = a * lax.rsqrt(jnp.mean(a * a, axis=-1, keepdims=True) + RMS_EPS) * g_ref[...]
                o_ref[:, c * HEAD_DIM:(c + 1) * HEAD_DIM] = y.astype(o_ref.dtype)

        @pl.when(j >= n_norm_tiles)
        def _():
            o_ref[...] = acc.astype(o_ref.dtype)


def matmul(x, w, n_out, out_dtype, tm, tn, mode="plain", resid=None, gain=None, n_norm_tiles=0):
    m, k = x.shape
    in_specs = [pl.BlockSpec((tm, k), lambda j, i: (i, 0)), pl.BlockSpec((k, tn), lambda j, i: (0, j))]
    args = [x, w]
    if mode == "resid":
        in_specs.append(pl.BlockSpec((tm, tn), lambda j, i: (i, j)))
        args.append(resid)
    elif mode == "headnorm":
        in_specs.append(pl.BlockSpec((1, HEAD_DIM), lambda j, i: (0, 0)))
        args.append(gain.reshape(1, HEAD_DIM))
    return pl.pallas_call(
        functools.partial(_mm_body, mode=mode, n_norm_tiles=n_norm_tiles),
        grid=(n_out // tn, m // tm),
        in_specs=in_specs,
        out_specs=pl.BlockSpec((tm, tn), lambda j, i: (i, j)),
        out_shape=jax.ShapeDtypeStruct((m, n_out), out_dtype),
        scratch_shapes=[pltpu.VMEM((k, tn), BF16)],
        compiler_params=_cparams("arbitrary", "arbitrary"), name="matmul_" + mode)(*args)


def _gates_body(x_ref, w_ref, b_ref, o_ref):
    pre = _dot(x_ref[...], w_ref[...].astype(BF16)) + b_ref[...]
    lane = lax.broadcasted_iota(jnp.int32, pre.shape, 1)
    o_ref[...] = jnp.where(lane < A_HEADS, pre, jnp.where(lane < 2 * A_HEADS, _log_sigmoid(pre), 0.0))


def mlstm_gates(x, w_gate, b_gate, tm):
    m, k = x.shape
    w = jnp.pad(w_gate, ((0, 0), (0, LANES - w_gate.shape[1])))
    b = jnp.pad(b_gate.astype(F32), (0, LANES - b_gate.shape[0])).reshape(1, LANES)
    return pl.pallas_call(
        _gates_body, grid=(m // tm,),
        in_specs=[pl.BlockSpec((tm, k), lambda i: (i, 0)), pl.BlockSpec((k, LANES), lambda i: (0, 0)),
                  pl.BlockSpec((1, LANES), lambda i: (0, 0))],
        out_specs=pl.BlockSpec((tm, LANES), lambda i: (i, 0)),
        out_shape=jax.ShapeDtypeStruct((m, LANES), F32),
        compiler_params=_cparams("arbitrary"), name="mlstm_gates")(x, w, b)


def _mlstm_body(*refs, chunk, lb, t_valid, t_total, has_init):
    if has_init:
        q_ref, k_ref, v_ref, o_ref, ga_ref, gh_ref, c0_ref, n0_ref, m0_ref, hg_ref, c_ref, n_ref, m_ref = refs
    else:
        q_ref, k_ref, v_ref, o_ref, ga_ref, gh_ref, hg_ref, c_ref, n_ref, m_ref = refs
    h = pl.program_id(1)
    c = pl.program_id(2)

    @pl.when(c == 0)
    def _():
        if has_init:
            c_ref[...] = c0_ref[...]
            n_ref[...] = n0_ref[...]
            m_ref[...] = m0_ref[...]
        else:
            c_ref[...] = jnp.zeros_like(c_ref)
            n_ref[...] = jnp.zeros_like(n_ref)
            m_ref[...] = jnp.zeros_like(m_ref)

    def load(ref):
        x = ref[...]
        if lb == chunk:
            return x
        return jnp.concatenate([x, jnp.zeros((chunk - lb, x.shape[1]), x.dtype)], axis=0)

    q = load(q_ref)
    k = load(k_ref)
    v = load(v_ref)
    ga = load(ga_ref)
    c_state = c_ref[0, 0]
    n_state = n_ref[0, 0]
    m_state = m_ref[0, 0][:, 0:1]

    lane = lax.broadcasted_iota(jnp.int32, (chunk, LANES), 1)
    i_col = jnp.sum(jnp.where(lane == h, ga, 0.0), axis=1, keepdims=True)
    f_col = jnp.sum(jnp.where(lane == h + A_HEADS, ga, 0.0), axis=1, keepdims=True)
    if t_valid < t_total:
        t_idx = lax.broadcasted_iota(jnp.int32, (chunk, 1), 0) + c * chunk
        i_col = jnp.where(t_idx < t_valid, i_col, NEG_BIG)
        f_col = jnp.where(t_idx < t_valid, f_col, 0.0)

    row_i = lax.broadcasted_iota(jnp.int32, (chunk, chunk), 0)
    col_i = lax.broadcasted_iota(jnp.int32, (chunk, chunk), 1)
    causal = col_i <= row_i
    g2 = jnp.where(lane == 0, i_col, jnp.where(lane == 1, f_col, 0.0))
    cum = _dot(causal.astype(F32), g2, precision=HIGHEST)
    g3 = jnp.where(lane == 1, cum, g2)
    g3t = g3.T
    i_row = g3t[0:1, :]
    b_row = g3t[1:2, :]
    b_col = g3[:, 1:2]

    dmat = jnp.where(causal, b_col - b_row + i_row, NEG_BIG)
    inter = b_col + m_state
    m_t = jnp.maximum(inter, jnp.max(dmat, axis=1, keepdims=True))
    w_intra = jnp.exp(dmat - m_t)
    w_inter = jnp.exp(inter - m_t)
    scale = A_QK_DIM ** -0.5
    s = _dot_nt(q, k) * scale * w_intra
    num = w_inter * (_dot(q, c_state.astype(BF16)) * scale) + _dot(s.astype(BF16), v)
    qn = jnp.sum(q.astype(F32) * n_state, axis=1, keepdims=True) * scale
    den = w_inter * qn + jnp.sum(s, axis=1, keepdims=True)
    hh = num / jnp.maximum(jnp.abs(den), jnp.exp(-m_t))

    hn = hh * lax.rsqrt(jnp.mean(hh * hh, axis=1, keepdims=True) + RMS_EPS) * gh_ref[0]
    og = o_ref[...].astype(F32)
    out = hn[0:lb] * (1.0 / (1.0 + jnp.exp(-og)))
    hg_ref[...] = out.astype(hg_ref.dtype)

    b_last = b_col[chunk - 1:chunk, :]
    g = b_last - b_col + i_col
    m_new = jnp.maximum(b_last + m_state, jnp.max(g, axis=0, keepdims=True))
    wk = jnp.exp(g - m_new)
    decay = jnp.exp(b_last + m_state - m_new)
    kw = k.astype(F32) * wk
    c_ref[0, 0] = decay * c_state + _dot(kw.T.astype(BF16), v)
    n_ref[0, 0] = decay * n_state + jnp.sum(kw, axis=0, keepdims=True)
    m_ref[0, 0] = jnp.broadcast_to(m_new, (1, LANES))


def mlstm(p_main, gates, g_hnorm, hg_prev, row0, n_seq, t_total, t_valid, lb, chunk, init=None):
    m = p_main.shape[0]
    nc = t_total // lb
    rb0 = row0 // lb

    def rows(b, h, c):
        return rb0 + b * nc + c

    qk_blocks = A_QK // A_QK_DIM
    in_specs = [
        pl.BlockSpec((lb, A_QK_DIM), lambda b, h, c: (rows(b, h, c), h)),
        pl.BlockSpec((lb, A_QK_DIM), lambda b, h, c: (rows(b, h, c), qk_blocks + h)),
        pl.BlockSpec((lb, A_V_DIM), lambda b, h, c: (rows(b, h, c), 2 * A_QK // A_V_DIM + h)),
        pl.BlockSpec((lb, A_V_DIM), lambda b, h, c: (rows(b, h, c), (2 * A_QK + A_VD) // A_V_DIM + h)),
        pl.BlockSpec((lb, LANES), lambda b, h, c: (rows(b, h, c), 0)),
        pl.BlockSpec((1, 1, A_V_DIM), lambda b, h, c: (h, 0, 0)),
    ]
    args = [p_main, p_main, p_main, p_main, gates, g_hnorm.reshape(A_HEADS, 1, A_V_DIM)]
    state_specs = [
        pl.BlockSpec((1, 1, A_QK_DIM, A_V_DIM), lambda b, h, c: (b, h, 0, 0)),
        pl.BlockSpec((1, 1, 1, A_QK_DIM), lambda b, h, c: (b, h, 0, 0)),
        pl.BlockSpec((1, 1, 1, LANES), lambda b, h, c: (b, h, 0, 0)),
    ]
    has_init = init is not None
    if has_init:
        c0, n0, m0 = init
        in_specs += state_specs
        args += [c0.astype(F32), n0.astype(F32).reshape(n_seq, A_HEADS, 1, A_QK_DIM),
                 jnp.broadcast_to(m0.astype(F32)[:, :, None, None], (n_seq, A_HEADS, 1, LANES))]
    aliases = {}
    if hg_prev is not None:
        in_specs.append(pl.BlockSpec(memory_space=pl.ANY))
        args.append(hg_prev)
        aliases = {len(args) - 1: 0}

    def body(*refs):
        if hg_prev is not None:
            n_in = len(args)
            refs = refs[:n_in - 1] + refs[n_in:]
        _mlstm_body(*refs, chunk=chunk, lb=lb, t_valid=t_valid, t_total=t_total, has_init=has_init)

    hg, c_out, n_out, m_out = pl.pallas_call(
        body, grid=(n_seq, A_HEADS, nc),
        in_specs=in_specs,
        out_specs=[pl.BlockSpec((lb, A_V_DIM), lambda b, h, c: (rows(b, h, c), h))] + state_specs,
        out_shape=[jax.ShapeDtypeStruct((m, A_VD), BF16),
                   jax.ShapeDtypeStruct((n_seq, A_HEADS, A_QK_DIM, A_V_DIM), F32),
                   jax.ShapeDtypeStruct((n_seq, A_HEADS, 1, A_QK_DIM), F32),
                   jax.ShapeDtypeStruct((n_seq, A_HEADS, 1, LANES), F32)],
        input_output_aliases=aliases,
        compiler_params=_cparams("arbitrary", "arbitrary", "arbitrary"), name="mlstm")(*args)
    return hg, c_out, n_out.reshape(n_seq, A_HEADS, A_QK_DIM), m_out[:, :, 0, 0]


def _sb_block(z, tri, carry, mask):
    l1p = jnp.log(1.0 + jnp.exp(-jnp.abs(z)))
    ls = jnp.minimum(z, 0.0) - l1p
    u = ls - z
    if mask is not None:
        u = jnp.where(mask, u, 0.0)
    u_hi = u.astype(BF16)
    u_lo = (u - u_hi.astype(F32)).astype(BF16)
    within = _dot(u_hi, tri) + _dot(u_lo, tri)
    a = jnp.exp(ls + within + carry)
    if mask is not None:
        a = jnp.where(mask, a, 0.0)
    return a, carry + within[:, 0:1] + u[:, 0:1]


def _strict_tri(tk):
    r = lax.broadcasted_iota(jnp.int32, (tk, tk), 0)
    c = lax.broadcasted_iota(jnp.int32, (tk, tk), 1)
    return (r > c).astype(BF16)


def _attn_prompt_body(q_ref, k_ref, v_ref, b_ref, o_ref, *, tq, tk):
    qi = pl.program_id(2)
    scale = HEAD_DIM ** -0.5
    tri = _strict_tri(tk)
    t_loc = lax.broadcasted_iota(jnp.int32, (tq, tk), 0)
    s_loc = lax.broadcasted_iota(jnp.int32, (tq, tk), 1)
    nkb = tq // tk
    q_heads = [q_ref[:, g * HEAD_DIM:(g + 1) * HEAD_DIM] for g in range(Q_PER_KV)]
    bias = [b_ref[0, :, g:g + 1] for g in range(Q_PER_KV)]

    def visit(j, state, diag_off):
        start = pl.multiple_of(j * tk, tk)
        kb = k_ref[pl.ds(start, tk), :].astype(BF16)
        vb = v_ref[pl.ds(start, tk), :].astype(BF16)
        mask = None if diag_off is None else (s_loc + diag_off * tk) < t_loc
        new = []
        for g in range(Q_PER_KV):
            carry, acc = state[g]
            z = _dot_nt(q_heads[g], kb) * scale + bias[g]
            a, carry = _sb_block(z, tri, carry, mask)
            new.append((carry, acc + _dot(a.astype(BF16), vb)))
        return tuple(new)

    state = tuple((jnp.zeros((tq, 1), F32), jnp.zeros((tq, HEAD_DIM), F32)) for _ in range(Q_PER_KV))
    for d in reversed(range(nkb)):
        state = visit(qi * nkb + d, state, d)
    n_before = qi * nkb
    state = lax.fori_loop(0, n_before, lambda it, st: visit(n_before - 1 - it, st, None), state)
    for g in range(Q_PER_KV):
        o_ref[:, g * HEAD_DIM:(g + 1) * HEAD_DIM] = state[g][1].astype(o_ref.dtype)


def attn_prompt(q_all, kv_all, b_logit, tq=128, tk=128):
    m = q_all.shape[0]
    nq = SEQ // tq
    gw = Q_PER_KV * HEAD_DIM
    bias = b_logit.astype(F32).reshape(KV_HEADS, 1, Q_PER_KV)
    return pl.pallas_call(
        functools.partial(_attn_prompt_body, tq=tq, tk=tk),
        grid=(BATCH, KV_HEADS, nq),
        in_specs=[pl.BlockSpec((tq, gw), lambda b, h, i: (b * nq + i, h)),
                  pl.BlockSpec((SEQ, HEAD_DIM), lambda b, h, i: (b, h)),
                  pl.BlockSpec((SEQ, HEAD_DIM), lambda b, h, i: (b, KV_HEADS + h)),
                  pl.BlockSpec((1, 1, Q_PER_KV), lambda b, h, i: (h, 0, 0))],
        out_specs=pl.BlockSpec((tq, gw), lambda b, h, i: (b * nq + i, h)),
        out_shape=jax.ShapeDtypeStruct((m, B_HEADS * HEAD_DIM), BF16),
        compiler_params=_cparams("arbitrary", "arbitrary", "arbitrary"), name="attn_prompt")(q_all, kv_all, kv_all, bias)


QROWS = 8
PAGES_PER_STEP = 2


def _attn_sample_body(pt_ref, q_ref, kn_ref, vn_ref, *rest, n_steps):
    ck_refs = rest[:PAGES_PER_STEP]
    cv_refs = rest[PAGES_PER_STEP:2 * PAGES_PER_STEP]
    b_ref, oin_ref, o_ref, qs_ref, car_ref, acc_ref = rest[2 * PAGES_PER_STEP:]
    del pt_ref, oin_ref
    p = pl.program_id(1)
    scale = HEAD_DIM ** -0.5
    rows_h = Q_PER_KV * QROWS
    tri = _strict_tri(PAGE_SIZE)

    def process(get_k, get_v, mask, car, acc):
        qs = qs_ref[...]
        zs = [_dot_nt(qs[kh * rows_h:(kh + 1) * rows_h].astype(BF16), get_k(kh).astype(BF16)) for kh in range(KV_HEADS)]
        z = jnp.concatenate(zs, axis=0) * scale + b_ref[...]
        a, car = _sb_block(z, tri, car, mask)
        ab = a.astype(BF16)
        accs = [acc[kh * rows_h:(kh + 1) * rows_h] + _dot(ab[kh * rows_h:(kh + 1) * rows_h], get_v(kh).astype(BF16))
                for kh in range(KV_HEADS)]
        return car, jnp.concatenate(accs, axis=0)

    @pl.when(p == 0)
    def _():
        for hq in range(B_HEADS):
            qs_ref[hq * QROWS:(hq + 1) * QROWS, :] = q_ref[:, hq * HEAD_DIM:(hq + 1) * HEAD_DIM].astype(F32)[0:QROWS]
        pad = jnp.zeros((PAGE_SIZE - S_PAD, HEAD_DIM), F32)
        n_rows = B_HEADS * QROWS
        t_loc = jnp.bitwise_and(lax.broadcasted_iota(jnp.int32, (n_rows, PAGE_SIZE), 0), QROWS - 1)
        s_loc = lax.broadcasted_iota(jnp.int32, (n_rows, PAGE_SIZE), 1)
        car, acc = process(lambda kh: jnp.concatenate([kn_ref[:, kh * HEAD_DIM:(kh + 1) * HEAD_DIM], pad], axis=0),
                           lambda kh: jnp.concatenate([vn_ref[:, kh * HEAD_DIM:(kh + 1) * HEAD_DIM], pad], axis=0),
                           s_loc < t_loc, jnp.zeros((n_rows, 1), F32), jnp.zeros((n_rows, HEAD_DIM), F32))
        car_ref[...] = car
        acc_ref[...] = acc

    car = car_ref[...]
    acc = acc_ref[...]
    for ck_ref, cv_ref in zip(ck_refs, cv_refs):
        car, acc = process(lambda kh: ck_ref[0, pl.ds(kh, PAGE_SIZE, stride=KV_HEADS), :],
                           lambda kh: cv_ref[0, pl.ds(kh, PAGE_SIZE, stride=KV_HEADS), :], None, car, acc)
    car_ref[...] = car
    acc_ref[...] = acc

    @pl.when(p == n_steps - 1)
    def _():
        pad = jnp.zeros((S_PAD - QROWS, HEAD_DIM), F32)
        for hq in range(B_HEADS):
            full = jnp.concatenate([acc[hq * QROWS:(hq + 1) * QROWS, :], pad], axis=0)
            o_ref[:, hq * HEAD_DIM:(hq + 1) * HEAD_DIM] = full.astype(o_ref.dtype)


def attn_sample(q_all, kv_all, cache_k, cache_v, page_table, b_logit, o_prev):
    n_seq, n_pages = page_table.shape
    n_steps = n_pages // PAGES_PER_STEP
    rb0 = MP // S_PAD
    kvw = KV_HEADS * HEAD_DIM
    n_phys = cache_k.shape[0]
    ck = cache_k.reshape(n_phys, PAGE_SIZE * KV_HEADS, HEAD_DIM)
    cv = cache_v.reshape(n_phys, PAGE_SIZE * KV_HEADS, HEAD_DIM)
    bias = jnp.broadcast_to(b_logit.astype(F32)[:, None, None], (B_HEADS, QROWS, 1)).reshape(B_HEADS * QROWS, 1)

    def page(u):
        return lambda s, p, pt: (pt[s * n_pages + n_pages - 1 - (p * PAGES_PER_STEP + u)], 0, 0)

    page_specs = [pl.BlockSpec((1, PAGE_SIZE * KV_HEADS, HEAD_DIM), page(u)) for u in range(PAGES_PER_STEP)]
    grid_spec = pltpu.PrefetchScalarGridSpec(
        num_scalar_prefetch=1, grid=(n_seq, n_steps),
        in_specs=[pl.BlockSpec((S_PAD, B_HEADS * HEAD_DIM), lambda s, p, pt: (rb0 + s, 0)),
                  pl.BlockSpec((S_PAD, kvw), lambda s, p, pt: (rb0 + s, 0)),
                  pl.BlockSpec((S_PAD, kvw), lambda s, p, pt: (rb0 + s, 1))] + page_specs + page_specs + [
                  pl.BlockSpec((B_HEADS * QROWS, 1), lambda s, p, pt: (0, 0)),
                  pl.BlockSpec(memory_space=pl.ANY)],
        out_specs=pl.BlockSpec((S_PAD, B_HEADS * HEAD_DIM), lambda s, p, pt: (rb0 + s, 0)),
        scratch_shapes=[pltpu.VMEM((B_HEADS * QROWS, HEAD_DIM), F32),
                        pltpu.VMEM((B_HEADS * QROWS, 1), F32),
                        pltpu.VMEM((B_HEADS * QROWS, HEAD_DIM), F32)])
    args = [page_table.reshape(-1), q_all, kv_all, kv_all] + [ck] * PAGES_PER_STEP + [cv] * PAGES_PER_STEP + [bias, o_prev]
    return pl.pallas_call(
        functools.partial(_attn_sample_body, n_steps=n_steps),
        grid_spec=grid_spec,
        out_shape=jax.ShapeDtypeStruct(o_prev.shape, o_prev.dtype),
        input_output_aliases={len(args) - 1: 0},
        compiler_params=_cparams("arbitrary", "arbitrary"), name="attn_sample")(*args)


PACK_CHUNKS = D_MODEL // 2 // LANES
ROW_CHUNKS = D_MODEL // LANES
HI_MASK = 0xFFFF0000


def _router_body(h_ref, g_ref, w_ref, xp_ref, idx_ref, gate_ref):
    x = h_ref[...]
    tr = x.shape[0]
    xn = x * lax.rsqrt(jnp.mean(x * x, axis=-1, keepdims=True) + RMS_EPS) * g_ref[...]
    bits = pltpu.bitcast(xn.astype(BF16).astype(F32), jnp.uint32)
    packed = bits[:, :D_MODEL // 2] | (bits[:, D_MODEL // 2:] >> 16)
    for c in range(PACK_CHUNKS):
        xp_ref[pl.ds(c, tr, stride=PACK_CHUNKS), :] = packed[:, c * LANES:(c + 1) * LANES]
    logits = _dot(xn, w_ref[...], precision=HIGHEST)
    lane = lax.broadcasted_iota(jnp.int32, logits.shape, 1)
    lane_f = lane.astype(F32)
    ninf = -jnp.inf

    def first_max(vals):
        mx = jnp.max(vals, axis=1, keepdims=True)
        return mx, jnp.min(jnp.where(vals == mx, lane_f, float(LANES)), axis=1, keepdims=True)

    gl = jnp.where(lane < N_GROUPS, logits, ninf)
    gmax, grp = first_max(gl)
    g_gate = 1.0 / jnp.sum(jnp.exp(gl - gmax), axis=1, keepdims=True)
    e_lane = lane - N_GROUPS
    lane_grp = jnp.right_shift(e_lane, 3).astype(F32)
    in_grp = (e_lane >= 0) & (e_lane < N_EXPERTS) & (lane_grp == grp)
    el = jnp.where(in_grp, logits, ninf)
    t1, i1 = first_max(el)
    t2, i2 = first_max(jnp.where(lane_f == i1, ninf, el))
    e21 = jnp.exp(t2 - t1)
    g1 = g_gate / (1.0 + e21)
    g2 = g_gate * e21 / (1.0 + e21)
    idx_ref[...] = jnp.where(lane == 0, i1, jnp.where(lane == 1, i2, float(N_GROUPS))).astype(jnp.int32) - N_GROUPS
    gate_ref[...] = jnp.where(lane == 0, g1, jnp.where(lane == 1, g2, 0.0))


def router(h, g_norm, w_group, w_router, tr):
    m, d = h.shape
    w = jnp.pad(jnp.concatenate([w_group, w_router], axis=1).astype(F32), ((0, 0), (0, LANES - N_GROUPS - N_EXPERTS)))
    small = pl.BlockSpec((tr, LANES), lambda i: (i, 0))
    xp, idx, gate = pl.pallas_call(
        _router_body, grid=(m // tr,),
        in_specs=[pl.BlockSpec((tr, d), lambda i: (i, 0)), pl.BlockSpec((1, d), lambda i: (0, 0)),
                  pl.BlockSpec((d, LANES), lambda i: (0, 0))],
        out_specs=[pl.BlockSpec((tr * PACK_CHUNKS, LANES), lambda i: (i, 0)), small, small],
        out_shape=[jax.ShapeDtypeStruct((m * PACK_CHUNKS, LANES), jnp.uint32),
                   jax.ShapeDtypeStruct((m, LANES), jnp.int32), jax.ShapeDtypeStruct((m, LANES), F32)],
        compiler_params=_cparams("arbitrary"), name="router")(h, g_norm.reshape(1, d), w)
    return xp, idx[:, :2], gate[:, :2]


BM = 1024
SUB = 256
UNSORT_CHUNK = 256


def _moe_plan(idx, gates):
    m = idx.shape[0]
    flat_e = idx.reshape(-1)
    onehot = (flat_e[:, None] == jnp.arange(N_EXPERTS, dtype=jnp.int32)[None, :]).astype(jnp.int32)
    csum = jnp.cumsum(onehot, axis=0)
    rank = jnp.take_along_axis(csum, flat_e[:, None], axis=1)[:, 0] - 1
    counts = csum[-1]
    nblk = (counts + BM - 1) // BM
    ends = jnp.cumsum(nblk)
    bstart = ends - nblk
    dest = (bstart[flat_e] * BM + rank).astype(jnp.int32)
    n_blocks = N_EXPERTS + (2 * m) // BM
    n_used = ends[-1]
    used = jnp.arange(n_blocks) < n_used
    blk = jnp.minimum(jnp.arange(n_blocks, dtype=jnp.int32), n_used - 1)
    be = jnp.minimum(jnp.searchsorted(ends, blk, side="right"), N_EXPERTS - 1).astype(jnp.int32)
    cnt = jnp.clip(counts[be] - (blk - bstart[be]) * BM, 0, BM)
    cnt = jnp.where(used, cnt, 0).astype(jnp.int32)
    n_rows = n_blocks * BM
    tok = jnp.arange(2 * m, dtype=jnp.int32) // 2
    row_src = jnp.zeros((n_rows,), jnp.int32).at[dest].set(tok)
    row_gate = jnp.zeros((n_rows,), F32).at[dest].set(gates.reshape(-1))
    dest_by_slot = dest.reshape(m, 2).T.reshape(-1)
    gather_lim = ((cnt + SUB - 1) // SUB) * SUB
    return dict(be=be, bx=blk.astype(jnp.int32), cnt=cnt, row_src=row_src, row_gate=row_gate,
                dest_by_slot=dest_by_slot, gather_lim=gather_lim, n_blocks=n_blocks, n_rows=n_rows)


def _copy_rows_body(lim_ref, idx_ref, src_ref, dst_ref, sem, *, chunk, rs):
    i = pl.program_id(0)
    base = i * chunk
    n = lim_ref[i]

    def desc(r):
        src_row = pl.multiple_of(idx_ref[0, 0, r] * rs, rs)
        dst_row = pl.multiple_of((base + r) * rs, rs)
        return pltpu.make_async_copy(src_ref.at[pl.ds(src_row, rs)], dst_ref.at[pl.ds(dst_row, rs)], sem)

    def start(r, carry):
        desc(r).start()
        return carry

    def wait(r, carry):
        desc(r).wait()
        return carry

    lax.fori_loop(0, n, start, 0)
    lax.fori_loop(0, n, wait, 0)


def copy_rows(src, row_idx, limits, chunk, rs):
    n = row_idx.shape[0]
    return pl.pallas_call(
        functools.partial(_copy_rows_body, chunk=chunk, rs=rs),
        grid_spec=pltpu.PrefetchScalarGridSpec(
            num_scalar_prefetch=1, grid=(n // chunk,),
            in_specs=[pl.BlockSpec((1, 1, chunk), lambda i, lim: (i, 0, 0), memory_space=pltpu.SMEM),
                      pl.BlockSpec(memory_space=pl.ANY)],
            out_specs=pl.BlockSpec(memory_space=pl.ANY),
            scratch_shapes=[pltpu.SemaphoreType.DMA(())]),
        out_shape=jax.ShapeDtypeStruct((n * rs, LANES), src.dtype),
        compiler_params=_cparams("arbitrary"), name="copy_rows")(limits, row_idx.reshape(n // chunk, 1, chunk), src)


def _ffn_in_body(be_ref, bx_ref, cnt_ref, x_ref, wg_ref, wu_ref, o_ref, xs_ref):
    del be_ref, bx_ref
    b = pl.program_id(0)
    cnt = cnt_ref[b]

    @pl.when((pl.program_id(1) == 0) & (cnt > 0))
    def _():
        half = D_MODEL // 2
        for c in range(PACK_CHUNKS):
            w = x_ref[pl.ds(c, BM, stride=PACK_CHUNKS), :]
            xs_ref[:, c * LANES:(c + 1) * LANES] = pltpu.bitcast(w & jnp.uint32(HI_MASK), F32).astype(BF16)
            xs_ref[:, half + c * LANES:half + (c + 1) * LANES] = pltpu.bitcast(w << 16, F32).astype(BF16)

    @pl.when(cnt > 0)
    def _():
        wg = wg_ref[0, 0].astype(BF16)
        wu = wu_ref[0, 0].astype(BF16)
        for sub in range(BM // SUB):
            @pl.when(sub * SUB < cnt)
            def _():
                x = xs_ref[sub * SUB:(sub + 1) * SUB, :]
                g = _dot(x, wg)
                u = _dot(x, wu)
                o_ref[sub * SUB:(sub + 1) * SUB, :] = (g * (1.0 / (1.0 + jnp.exp(-g))) * u).astype(o_ref.dtype)


def _ffn_out_body(be_ref, bx_ref, cnt_ref, h_ref, w_ref, gate_ref, o_ref, wb_ref):
    del bx_ref
    b = pl.program_id(1)
    cnt = cnt_ref[b]
    new_w = (b == 0) | (be_ref[b] != be_ref[jnp.maximum(b - 1, 0)])

    @pl.when(new_w)
    def _():
        wb_ref[...] = w_ref[0, 0].astype(BF16)

    n_chunks = o_ref.shape[1]
    o2 = o_ref.reshape(BM * n_chunks, LANES)
    for sub in range(BM // SUB):
        @pl.when(sub * SUB < cnt)
        def _():
            y = _dot(h_ref[sub * SUB:(sub + 1) * SUB, :], wb_ref[...]) * gate_ref[sub * SUB:(sub + 1) * SUB, 0:1]
            for c in range(n_chunks):
                o2[pl.ds(sub * SUB * n_chunks + c, SUB, stride=n_chunks), :] = y[:, c * LANES:(c + 1) * LANES]


def expert_ffn(x_sorted, plan, w_in, w_out, layer, tn_in=256, tn_out=1024):
    n_rows = plan["n_rows"]
    nb = plan["n_blocks"]
    d = D_MODEL
    nci = D_EXPERT // tn_in

    def col(c, b, cnt):
        return jnp.where(cnt[b] > 0, c, nci - 1)

    h_mid = pl.pallas_call(
        _ffn_in_body,
        grid_spec=pltpu.PrefetchScalarGridSpec(
            num_scalar_prefetch=3, grid=(nb, nci),
            in_specs=[pl.BlockSpec((BM * PACK_CHUNKS, LANES), lambda b, c, be, bx, cnt: (bx[b], 0)),
                      pl.BlockSpec((1, 1, d, tn_in), lambda b, c, be, bx, cnt: (layer, be[b], 0, col(c, b, cnt))),
                      pl.BlockSpec((1, 1, d, tn_in), lambda b, c, be, bx, cnt: (layer, be[b], 0, nci + col(c, b, cnt)))],
            out_specs=pl.BlockSpec((BM, tn_in), lambda b, c, be, bx, cnt: (bx[b], col(c, b, cnt))),
            scratch_shapes=[pltpu.VMEM((BM, d), BF16)]),
        out_shape=jax.ShapeDtypeStruct((n_rows, D_EXPERT), BF16),
        compiler_params=_cparams("arbitrary", "arbitrary"), name="ffn_in")(
            plan["be"], plan["bx"], plan["cnt"], x_sorted, w_in, w_in)
    gate = jnp.broadcast_to(plan["row_gate"][:, None], (n_rows, LANES))
    nco = d // tn_out
    oc = tn_out // LANES
    return pl.pallas_call(
        _ffn_out_body,
        grid_spec=pltpu.PrefetchScalarGridSpec(
            num_scalar_prefetch=3, grid=(nco, nb),
            in_specs=[pl.BlockSpec((BM, D_EXPERT), lambda c, b, be, bx, cnt: (bx[b], 0)),
                      pl.BlockSpec((1, 1, D_EXPERT, tn_out), lambda c, b, be, bx, cnt: (layer, be[b], 0, c)),
                      pl.BlockSpec((BM, LANES), lambda c, b, be, bx, cnt: (bx[b], 0))],
            out_specs=pl.BlockSpec((BM, oc, LANES), lambda c, b, be, bx, cnt: (bx[b], c, 0)),
            scratch_shapes=[pltpu.VMEM((D_EXPERT, tn_out), BF16)]),
        out_shape=jax.ShapeDtypeStruct((n_rows, ROW_CHUNKS, LANES), F32),
        compiler_params=_cparams("arbitrary", "arbitrary"), name="ffn_out")(
            plan["be"], plan["bx"], plan["cnt"], h_mid, w_out, gate)


def _combine_body(h_ref, y_ref, *rest, n_norm):
    tr = h_ref.shape[0]
    if n_norm:
        g_ref, o_ref, *norm_refs = rest
    else:
        (o_ref,) = rest
    ss = jnp.zeros((tr, 1), F32)
    for c in range(ROW_CHUNKS):
        cols = slice(c * LANES, (c + 1) * LANES)
        hn = h_ref[:, cols] + y_ref[0, pl.ds(c, tr, stride=ROW_CHUNKS), :] + y_ref[1, pl.ds(c, tr, stride=ROW_CHUNKS), :]
        o_ref[:, cols] = hn
        ss = ss + jnp.sum(hn * hn, axis=1, keepdims=True)
    if n_norm:
        r = lax.rsqrt(ss * (1.0 / D_MODEL) + RMS_EPS)
        for c in range(ROW_CHUNKS):
            cols = slice(c * LANES, (c + 1) * LANES)
            xr = o_ref[:, cols] * r
            for j, n_ref in enumerate(norm_refs):
                n_ref[:, cols] = (xr * g_ref[j:j + 1, cols]).astype(n_ref.dtype)


def combine(h, y2, gains, tr):
    m, d = h.shape
    n_norm = 0 if gains is None else gains.shape[0]
    row = pl.BlockSpec((tr, d), lambda i: (i, 0))
    in_specs = [row, pl.BlockSpec((2, tr * ROW_CHUNKS, LANES), lambda i: (0, i, 0))]
    args = [h, y2.reshape(2, m * ROW_CHUNKS, LANES)]
    if n_norm:
        in_specs.append(pl.BlockSpec((n_norm, d), lambda i: (0, 0)))
        args.append(gains)
    return pl.pallas_call(
        functools.partial(_combine_body, n_norm=n_norm), grid=(m // tr,),
        in_specs=in_specs, out_specs=[row] * (1 + n_norm),
        out_shape=[jax.ShapeDtypeStruct((m, d), F32)] + [jax.ShapeDtypeStruct((m, d), BF16)] * n_norm,
        compiler_params=_cparams("arbitrary"), name="combine")(*args)


def hier_moe_layer(h, g_norm, w_group, w_router, w_in, w_out, layer, next_gains, tr):
    xp, idx, gates = router(h, g_norm, w_group, w_router, tr)
    plan = _moe_plan(idx, gates)
    x_sorted = copy_rows(xp, plan["row_src"], plan["gather_lim"], BM, PACK_CHUNKS)
    y_sorted = expert_ffn(x_sorted, plan, w_in, w_out, layer)
    n_back = plan["dest_by_slot"].shape[0]
    y2 = copy_rows(y_sorted.reshape(plan["n_rows"] * ROW_CHUNKS, LANES), plan["dest_by_slot"],
                   jnp.full((n_back // UNSORT_CHUNK,), UNSORT_CHUNK, jnp.int32), UNSORT_CHUNK, ROW_CHUNKS)
    return combine(h, y2, next_gains, tr)


TM = 640
TR = 128


def kernel(x_prompt, x_sample, state_C, state_n, state_m, cache_k, cache_v, page_table, norm_mix, w_in_a, b_gate_a,
           g_hnorm_a, w_out_a, g_kv, w_kv, g_knorm, w_q_b, g_qnorm_b, b_logit_b, w_out_b, norm_ffn, w_group,
           w_router, w_moe_in, w_moe_out):
    xs = jnp.pad(x_sample, ((0, 0), (0, S_PAD - DEC_SEQ), (0, 0))).reshape(MS, D_MODEL)
    h0 = jnp.concatenate([x_prompt.reshape(MP, D_MODEL), xs], axis=0)

    (xn0,) = rms_norm(h0, norm_mix[0:1], [BF16], TR)
    p_main = matmul(xn0, w_in_a[0], A_MAIN, BF16, TM, 512)
    gates = mlstm_gates(xn0, w_in_a[0][:, A_MAIN:], b_gate_a[0], TM)
    hg, pc, pn, pm = mlstm(p_main, gates, g_hnorm_a[0], None, 0, BATCH, SEQ, SEQ, 256, 256)
    hg, sc, sn, sm = mlstm(p_main, gates, g_hnorm_a[0], hg, MP, DEC_BATCH, S_PAD, DEC_SEQ, S_PAD, LANES,
                           init=(state_C[0], state_n[0], state_m[0]))
    h1 = matmul(hg, w_out_a[0], D_MODEL, F32, TM, 512, mode="resid", resid=h0)
    h1, xn1, xkv = hier_moe_layer(h1, norm_ffn[0], w_group[0], w_router[0], w_moe_in, w_moe_out, 0,
                                  jnp.stack([norm_mix[1], g_kv]), TR)

    kv_all = matmul(xkv, w_kv, 2 * KV_HEADS * HEAD_DIM, F32, TM, 512, mode="headnorm", gain=g_knorm,
                    n_norm_tiles=KV_HEADS * HEAD_DIM // 512)
    q_all = matmul(xn1, w_q_b[0], B_HEADS * HEAD_DIM, BF16, TM, 512, mode="headnorm", gain=g_qnorm_b[0],
                   n_norm_tiles=B_HEADS * HEAD_DIM // 512)
    o_all = attn_prompt(q_all, kv_all, b_logit_b[0])
    o_all = attn_sample(q_all, kv_all, cache_k, cache_v, page_table, b_logit_b[0], o_all)
    h2 = matmul(o_all, w_out_b[0], D_MODEL, F32, TM, 512, mode="resid", resid=h1)
    (h3,) = hier_moe_layer(h2, norm_ffn[1], w_group[1], w_router[1], w_moe_in, w_moe_out, 1, None, TR)

    def sample_rows(a):
        return a[MP:].reshape((DEC_BATCH, S_PAD) + a.shape[1:])[:, :DEC_SEQ]

    kvw = KV_HEADS * HEAD_DIM
    y_prompt = h3[:MP].reshape(BATCH, SEQ, D_MODEL)
    y_sample = sample_rows(h3)
    prompt_k = kv_all[:MP, :kvw].reshape(BATCH, SEQ, KV_HEADS, HEAD_DIM)
    prompt_v = kv_all[:MP, kvw:].reshape(BATCH, SEQ, KV_HEADS, HEAD_DIM)
    sample_k = sample_rows(kv_all[:, :kvw]).reshape(DEC_BATCH, DEC_SEQ, KV_HEADS, HEAD_DIM)
    sample_v = sample_rows(kv_all[:, kvw:]).reshape(DEC_BATCH, DEC_SEQ, KV_HEADS, HEAD_DIM)
    return (y_prompt, y_sample, pc[None], pn[None], pm[None], prompt_k, prompt_v,
            sc[None], sn[None], sm[None], sample_k, sample_v)
```

```python
import functools

import jax
import jax.numpy as jnp
from jax import lax
from jax.experimental import pallas as pl
from jax.experimental.pallas import tpu as pltpu

F32 = jnp.float32
BF16 = jnp.bfloat16

D_MODEL = 4096
BATCH = 4
SEQ = 2048
DEC_BATCH = 8
DEC_SEQ = 4
PAGE_SIZE = 128
A_HEADS = 8
A_QK_DIM = 256
A_V_DIM = 512
A_QK = A_HEADS * A_QK_DIM
A_VD = A_HEADS * A_V_DIM
A_MAIN = 2 * A_QK + A_VD + D_MODEL
HEAD_DIM = 128
B_HEADS = 32
KV_HEADS = 8
Q_PER_KV = B_HEADS // KV_HEADS
N_GROUPS = 4
EXPERTS_PER_GROUP = 8
N_EXPERTS = N_GROUPS * EXPERTS_PER_GROUP
D_EXPERT = 1024
RMS_EPS = 1e-6

LANES = 128
MP = BATCH * SEQ
S_PAD = 16
MS = DEC_BATCH * S_PAD
M_ALL = MP + MS
NEG_BIG = -1e30
VMEM_LIMIT = 56 * 1024 * 1024

HIGHEST = lax.Precision.HIGHEST


def _cparams(*sem):
    return pltpu.CompilerParams(dimension_semantics=sem, vmem_limit_bytes=VMEM_LIMIT)


def _dot(a, b, precision=None):
    return jnp.dot(a, b, preferred_element_type=F32, precision=precision)


def _dot_nt(a, b):
    return lax.dot_general(a, b, (((1,), (1,)), ((), ())), preferred_element_type=F32)


def _log_sigmoid(x):
    return jnp.minimum(x, 0.0) - jnp.log1p(jnp.exp(-jnp.abs(x)))


def _rms_body(x_ref, g_ref, *out_refs):
    x = x_ref[...].astype(F32)
    xr = x * lax.rsqrt(jnp.mean(x * x, axis=-1, keepdims=True) + RMS_EPS)
    for j, o_ref in enumerate(out_refs):
        o_ref[...] = (xr * g_ref[j:j + 1, :]).astype(o_ref.dtype)


def rms_norm(x, gains, out_dtypes, tr):
    m, d = x.shape
    n = len(out_dtypes)
    row = pl.BlockSpec((tr, d), lambda i: (i, 0))
    return pl.pallas_call(
        _rms_body, grid=(m // tr,),
        in_specs=[row, pl.BlockSpec((n, d), lambda i: (0, 0))],
        out_specs=[row] * n,
        out_shape=[jax.ShapeDtypeStruct((m, d), dt) for dt in out_dtypes],
        compiler_params=_cparams("arbitrary"), name="rms_norm")(x, gains)


def _mm_body(x_ref, w_ref, *rest, mode, n_norm_tiles):
    if mode == "resid":
        r_ref, o_ref, wb_ref = rest
    elif mode == "headnorm":
        g_ref, o_ref, wb_ref = rest
    else:
        o_ref, wb_ref = rest
    j = pl.program_id(0)

    @pl.when(pl.program_id(1) == 0)
    def _():
        wb_ref[...] = w_ref[...].astype(BF16)

    acc = _dot(x_ref[...], wb_ref[...])
    if mode == "plain":
        o_ref[...] = acc.astype(o_ref.dtype)
    elif mode == "resid":
        o_ref[...] = r_ref[...] + acc
    else:
        @pl.when(j < n_norm_tiles)
        def _():
            for c in range(acc.shape[1] // HEAD_DIM):
                a = acc[:, c * HEAD_DIM:(c + 1) * HEAD_DIM]
                y = a * lax.rsqrt(jnp.mean(a * a, axis=-1, keepdims=True) + RMS_EPS) * g_ref[...]
                o_ref[:, c * HEAD_DIM:(c + 1) * HEAD_DIM] = y.astype(o_ref.dtype)

        @pl.when(j >= n_norm_tiles)
        def _():
            o_ref[...] = acc.astype(o_ref.dtype)


def matmul(x, w, n_out, out_dtype, tm, tn, mode="plain", resid=None, gain=None, n_norm_tiles=0):
    m, k = x.shape
    in_specs = [pl.BlockSpec((tm, k), lambda j, i: (i, 0)), pl.BlockSpec((k, tn), lambda j, i: (0, j))]
    args = [x, w]
    if mode == "resid":
        in_specs.append(pl.BlockSpec((tm, tn), lambda j, i: (i, j)))
        args.append(resid)
    elif mode == "headnorm":
        in_specs.append(pl.BlockSpec((1, HEAD_DIM), lambda j, i: (0, 0)))
        args.append(gain.reshape(1, HEAD_DIM))
    return pl.pallas_call(
        functools.partial(_mm_body, mode=mode, n_norm_tiles=n_norm_tiles),
        grid=(n_out // tn, m // tm),
        in_specs=in_specs,
        out_specs=pl.BlockSpec((tm, tn), lambda j, i: (i, j)),
        out_shape=jax.ShapeDtypeStruct((m, n_out), out_dtype),
        scratch_shapes=[pltpu.VMEM((k, tn), BF16)],
        compiler_params=_cparams("arbitrary", "arbitrary"), name="matmul_" + mode)(*args)


def _gates_body(x_ref, w_ref, b_ref, o_ref):
    pre = _dot(x_ref[...], w_ref[...].astype(BF16)) + b_ref[...]
    lane = lax.broadcasted_iota(jnp.int32, pre.shape, 1)
    o_ref[...] = jnp.where(lane < A_HEADS, pre, jnp.where(lane < 2 * A_HEADS, _log_sigmoid(pre), 0.0))


def mlstm_gates(x, w_gate, b_gate, tm):
    m, k = x.shape
    w = jnp.pad(w_gate, ((0, 0), (0, LANES - w_gate.shape[1])))
    b = jnp.pad(b_gate.astype(F32), (0, LANES - b_gate.shape[0])).reshape(1, LANES)
    return pl.pallas_call(
        _gates_body, grid=(m // tm,),
        in_specs=[pl.BlockSpec((tm, k), lambda i: (i, 0)), pl.BlockSpec((k, LANES), lambda i: (0, 0)),
                  pl.BlockSpec((1, LANES), lambda i: (0, 0))],
        out_specs=pl.BlockSpec((tm, LANES), lambda i: (i, 0)),
        out_shape=jax.ShapeDtypeStruct((m, LANES), F32),
        compiler_params=_cparams("arbitrary"), name="mlstm_gates")(x, w, b)


MLSTM_SUB = 64
def _mlstm_body(*refs, chunk, lb, sub, t_valid, t_total, has_init):
    if has_init:
        q_ref, k_ref, v_ref, o_ref, ga_ref, gh_ref, c0_ref, n0_ref, m0_ref, hg_ref, c_ref, n_ref, m_ref = refs
    else:
        q_ref, k_ref, v_ref, o_ref, ga_ref, gh_ref, hg_ref, c_ref, n_ref, m_ref = refs
    h = pl.program_id(1)
    c = pl.program_id(2)

    @pl.when(c == 0)
    def _():
        if has_init:
            c_ref[...] = c0_ref[...]
            n_ref[...] = n0_ref[...]
            m_ref[...] = m0_ref[...]
        else:
            c_ref[...] = jnp.zeros_like(c_ref)
            n_ref[...] = jnp.zeros_like(n_ref)
            m_ref[...] = jnp.zeros_like(m_ref)

    def load(ref):
        x = ref[...]
        if lb == chunk:
            return x
        return jnp.concatenate([x, jnp.zeros((chunk - lb, x.shape[1]), x.dtype)], axis=0)

    q = load(q_ref).astype(BF16)
    k = load(k_ref).astype(BF16)
    v = load(v_ref).astype(BF16)
    ga = load(ga_ref)
    c_state = c_ref[0, 0]
    n_state = n_ref[0, 0]
    m_state = m_ref[0, 0][:, 0:1]
    n_sub = chunk // sub
    sub_shift = sub.bit_length() - 1

    lane = lax.broadcasted_iota(jnp.int32, (chunk, LANES), 1)
    i_col = jnp.sum(jnp.where(lane == h, ga, 0.0), axis=1, keepdims=True)
    f_col = jnp.sum(jnp.where(lane == h + A_HEADS, ga, 0.0), axis=1, keepdims=True)
    if t_valid < t_total:
        t_idx = lax.broadcasted_iota(jnp.int32, (chunk, 1), 0) + c * chunk
        i_col = jnp.where(t_idx < t_valid, i_col, NEG_BIG)
        f_col = jnp.where(t_idx < t_valid, f_col, 0.0)

    row_i = lax.broadcasted_iota(jnp.int32, (chunk, chunk), 0)
    col_i = lax.broadcasted_iota(jnp.int32, (chunk, chunk), 1)
    causal = (col_i <= row_i) & (jnp.right_shift(row_i, sub_shift) == jnp.right_shift(col_i, sub_shift))
    g2 = jnp.where(lane == 0, i_col, jnp.where(lane == 1, f_col, 0.0))
    cum = _dot(causal.astype(F32), g2, precision=HIGHEST)
    g3 = jnp.where(lane == 1, cum, g2)
    g3t = g3.T
    i_row = g3t[0:1, :]
    b_row = g3t[1:2, :]
    b_col = g3[:, 1:2]
    sub_of_row = jnp.right_shift(lax.broadcasted_iota(jnp.int32, (1, chunk), 1), sub_shift)
    sub_of_col = jnp.right_shift(lax.broadcasted_iota(jnp.int32, (chunk, 1), 0), sub_shift)

    m_at = [m_state]
    b_last = []
    for j in range(n_sub):
        bl = b_col[(j + 1) * sub - 1:(j + 1) * sub, :]
        g = bl - b_col[j * sub:(j + 1) * sub, :] + i_col[j * sub:(j + 1) * sub, :]
        b_last.append(bl)
        m_at.append(jnp.maximum(bl + m_at[j], jnp.max(g, axis=0, keepdims=True)))
    m_col = jnp.concatenate([jnp.broadcast_to(m_at[j], (sub, 1)) for j in range(n_sub)], axis=0)

    dmat = jnp.where(causal, b_col - b_row + i_row, NEG_BIG)
    inter = b_col + m_col
    m_t = jnp.maximum(inter, jnp.max(dmat, axis=1, keepdims=True))
    w_intra = jnp.exp(dmat - m_t)
    w_inter = jnp.exp(inter - m_t)
    scale = A_QK_DIM ** -0.5
    s = _dot_nt(q, k) * scale * w_intra
    intra = _dot(s.astype(BF16), v)
    s_sum = jnp.sum(s, axis=1, keepdims=True)

    kf = k.astype(F32)
    kt = kf.T
    qc_parts = []
    qn_parts = []
    c_cur = c_state
    n_cur = n_state
    for j in range(n_sub):
        qj = q[j * sub:(j + 1) * sub]
        qc_parts.append(_dot(qj, c_cur.astype(BF16)))
        qn_parts.append(jnp.sum(qj.astype(F32) * n_cur.astype(BF16).astype(F32), axis=1, keepdims=True))
        m_new = m_at[j + 1]
        wk_row = jnp.where(sub_of_row == j, jnp.exp(b_last[j] - b_row + i_row - m_new), 0.0)
        wk_col = jnp.where(sub_of_col == j, jnp.exp(b_last[j] - b_col + i_col - m_new), 0.0)
        decay = jnp.exp(b_last[j] + m_at[j] - m_new)
        c_cur = decay * c_cur + _dot((kt * wk_row).astype(BF16), v)
        n_cur = decay * n_cur + jnp.sum(kf * wk_col.astype(BF16).astype(F32), axis=0, keepdims=True)
    num = w_inter * (jnp.concatenate(qc_parts, axis=0) * scale) + intra
    den = w_inter * (jnp.concatenate(qn_parts, axis=0) * scale) + s_sum
    hh = num / jnp.maximum(jnp.abs(den), jnp.exp(-m_t))

    hn = hh * lax.rsqrt(jnp.mean(hh * hh, axis=1, keepdims=True) + RMS_EPS) * gh_ref[0]
    og = o_ref[...].astype(F32)
    out = hn[0:lb] * (1.0 / (1.0 + jnp.exp(-og)))
    hg_ref[...] = out.astype(hg_ref.dtype)

    c_ref[0, 0] = c_cur
    n_ref[0, 0] = n_cur
    m_ref[0, 0] = jnp.broadcast_to(m_at[n_sub], (1, LANES))


def mlstm(p_main, gates, g_hnorm, hg_prev, row0, n_seq, t_total, t_valid, lb, chunk, sub, init=None):
    m = p_main.shape[0]
    nc = t_total // lb
    rb0 = row0 // lb

    def rows(b, h, c):
        return rb0 + b * nc + c

    qk_blocks = A_QK // A_QK_DIM
    in_specs = [
        pl.BlockSpec((lb, A_QK_DIM), lambda b, h, c: (rows(b, h, c), h)),
        pl.BlockSpec((lb, A_QK_DIM), lambda b, h, c: (rows(b, h, c), qk_blocks + h)),
        pl.BlockSpec((lb, A_V_DIM), lambda b, h, c: (rows(b, h, c), 2 * A_QK // A_V_DIM + h)),
        pl.BlockSpec((lb, A_V_DIM), lambda b, h, c: (rows(b, h, c), (2 * A_QK + A_VD) // A_V_DIM + h)),
        pl.BlockSpec((lb, LANES), lambda b, h, c: (rows(b, h, c), 0)),
        pl.BlockSpec((1, 1, A_V_DIM), lambda b, h, c: (h, 0, 0)),
    ]
    args = [p_main, p_main, p_main, p_main, gates, g_hnorm.reshape(A_HEADS, 1, A_V_DIM)]
    state_specs = [
        pl.BlockSpec((1, 1, A_QK_DIM, A_V_DIM), lambda b, h, c: (b, h, 0, 0)),
        pl.BlockSpec((1, 1, 1, A_QK_DIM), lambda b, h, c: (b, h, 0, 0)),
        pl.BlockSpec((1, 1, 1, LANES), lambda b, h, c: (b, h, 0, 0)),
    ]
    has_init = init is not None
    if has_init:
        c0, n0, m0 = init
        in_specs += state_specs
        args += [c0.astype(F32), n0.astype(F32).reshape(n_seq, A_HEADS, 1, A_QK_DIM),
                 jnp.broadcast_to(m0.astype(F32)[:, :, None, None], (n_seq, A_HEADS, 1, LANES))]
    aliases = {}
    if hg_prev is not None:
        in_specs.append(pl.BlockSpec(memory_space=pl.ANY))
        args.append(hg_prev)
        aliases = {len(args) - 1: 0}

    def body(*refs):
        if hg_prev is not None:
            n_in = len(args)
            refs = refs[:n_in - 1] + refs[n_in:]
        _mlstm_body(*refs, chunk=chunk, lb=lb, sub=sub, t_valid=t_valid, t_total=t_total, has_init=has_init)

    hg, c_out, n_out, m_out = pl.pallas_call(
        body, grid=(n_seq, A_HEADS, nc),
        in_specs=in_specs,
        out_specs=[pl.BlockSpec((lb, A_V_DIM), lambda b, h, c: (rows(b, h, c), h))] + state_specs,
        out_shape=[jax.ShapeDtypeStruct((m, A_VD), BF16),
                   jax.ShapeDtypeStruct((n_seq, A_HEADS, A_QK_DIM, A_V_DIM), F32),
                   jax.ShapeDtypeStruct((n_seq, A_HEADS, 1, A_QK_DIM), F32),
                   jax.ShapeDtypeStruct((n_seq, A_HEADS, 1, LANES), F32)],
        input_output_aliases=aliases,
        compiler_params=_cparams("arbitrary", "arbitrary", "arbitrary"), name="mlstm")(*args)
    return hg, c_out, n_out.reshape(n_seq, A_HEADS, A_QK_DIM), m_out[:, :, 0, 0]


def _sb_block(z, tri, carry, mask):
    l1p = jnp.log(1.0 + jnp.exp(-jnp.abs(z)))
    ls = jnp.minimum(z, 0.0) - l1p
    u = ls - z
    if mask is not None:
        u = jnp.where(mask, u, 0.0)
    u_hi = u.astype(BF16)
    u_lo = (u - u_hi.astype(F32)).astype(BF16)
    within = _dot(u_hi, tri) + _dot(u_lo, tri)
    a = jnp.exp(ls + within + carry)
    if mask is not None:
        a = jnp.where(mask, a, 0.0)
    return a, carry + within[:, 0:1] + u[:, 0:1]


def _strict_tri(tk):
    r = lax.broadcasted_iota(jnp.int32, (tk, tk), 0)
    c = lax.broadcasted_iota(jnp.int32, (tk, tk), 1)
    return (r > c).astype(BF16)


def _attn_prompt_body(q_ref, k_ref, v_ref, b_ref, o_ref, *, tq, tk):
    qi = pl.program_id(2)
    scale = HEAD_DIM ** -0.5
    tri = _strict_tri(tk)
    t_loc = lax.broadcasted_iota(jnp.int32, (tq, tk), 0)
    s_loc = lax.broadcasted_iota(jnp.int32, (tq, tk), 1)
    nkb = tq // tk
    rows = Q_PER_KV * tq
    qs = jnp.concatenate([q_ref[:, g * HEAD_DIM:(g + 1) * HEAD_DIM] for g in range(Q_PER_KV)], axis=0)
    bias = jnp.concatenate([jnp.broadcast_to(b_ref[0, :, g:g + 1], (tq, 1)) for g in range(Q_PER_KV)], axis=0)

    def visit(j, state, diag_off):
        carry, acc = state
        start = pl.multiple_of(j * tk, tk)
        kb = k_ref[pl.ds(start, tk), :].astype(BF16)
        vb = v_ref[pl.ds(start, tk), :].astype(BF16)
        z = _dot_nt(qs, kb) * scale + bias
        mask = None
        if diag_off is not None:
            mask = jnp.concatenate([(s_loc + diag_off * tk) < t_loc] * Q_PER_KV, axis=0)
        a, carry = _sb_block(z, tri, carry, mask)
        return carry, acc + _dot(a.astype(BF16), vb)

    state = (jnp.zeros((rows, 1), F32), jnp.zeros((rows, HEAD_DIM), F32))
    for d in reversed(range(nkb)):
        state = visit(qi * nkb + d, state, d)
    n_before = qi * nkb
    state = lax.fori_loop(0, n_before, lambda it, st: visit(n_before - 1 - it, st, None), state)
    for g in range(Q_PER_KV):
        o_ref[:, g * HEAD_DIM:(g + 1) * HEAD_DIM] = state[1][g * tq:(g + 1) * tq].astype(o_ref.dtype)


def attn_prompt(q_all, kv_all, b_logit, tq=256, tk=128):
    m = q_all.shape[0]
    nq = SEQ // tq
    gw = Q_PER_KV * HEAD_DIM
    bias = b_logit.astype(F32).reshape(KV_HEADS, 1, Q_PER_KV)
    return pl.pallas_call(
        functools.partial(_attn_prompt_body, tq=tq, tk=tk),
        grid=(BATCH, KV_HEADS, nq),
        in_specs=[pl.BlockSpec((tq, gw), lambda b, h, i: (b * nq + i, h)),
                  pl.BlockSpec((SEQ, HEAD_DIM), lambda b, h, i: (b, h)),
                  pl.BlockSpec((SEQ, HEAD_DIM), lambda b, h, i: (b, KV_HEADS + h)),
                  pl.BlockSpec((1, 1, Q_PER_KV), lambda b, h, i: (h, 0, 0))],
        out_specs=pl.BlockSpec((tq, gw), lambda b, h, i: (b * nq + i, h)),
        out_shape=jax.ShapeDtypeStruct((m, B_HEADS * HEAD_DIM), BF16),
        compiler_params=_cparams("arbitrary", "arbitrary", "arbitrary"), name="attn_prompt")(q_all, kv_all, kv_all, bias)


QROWS = 8
PAGES_PER_STEP = 2


def _attn_sample_body(pt_ref, q_ref, kn_ref, vn_ref, *rest, n_steps):
    ck_refs = rest[:PAGES_PER_STEP]
    cv_refs = rest[PAGES_PER_STEP:2 * PAGES_PER_STEP]
    b_ref, oin_ref, o_ref, qs_ref, car_ref, acc_ref = rest[2 * PAGES_PER_STEP:]
    del pt_ref, oin_ref
    p = pl.program_id(1)
    scale = HEAD_DIM ** -0.5
    rows_h = Q_PER_KV * QROWS
    tri = _strict_tri(PAGE_SIZE)

    def process(get_k, get_v, mask, car, acc):
        qs = qs_ref[...]
        zs = [_dot_nt(qs[kh * rows_h:(kh + 1) * rows_h].astype(BF16), get_k(kh).astype(BF16)) for kh in range(KV_HEADS)]
        z = jnp.concatenate(zs, axis=0) * scale + b_ref[...]
        a, car = _sb_block(z, tri, car, mask)
        ab = a.astype(BF16)
        accs = [acc[kh * rows_h:(kh + 1) * rows_h] + _dot(ab[kh * rows_h:(kh + 1) * rows_h], get_v(kh).astype(BF16))
                for kh in range(KV_HEADS)]
        return car, jnp.concatenate(accs, axis=0)

    @pl.when(p == 0)
    def _():
        for hq in range(B_HEADS):
            qs_ref[hq * QROWS:(hq + 1) * QROWS, :] = q_ref[:, hq * HEAD_DIM:(hq + 1) * HEAD_DIM].astype(F32)[0:QROWS]
        pad = jnp.zeros((PAGE_SIZE - S_PAD, HEAD_DIM), F32)
        n_rows = B_HEADS * QROWS
        t_loc = jnp.bitwise_and(lax.broadcasted_iota(jnp.int32, (n_rows, PAGE_SIZE), 0), QROWS - 1)
        s_loc = lax.broadcasted_iota(jnp.int32, (n_rows, PAGE_SIZE), 1)
        car, acc = process(lambda kh: jnp.concatenate([kn_ref[:, kh * HEAD_DIM:(kh + 1) * HEAD_DIM], pad], axis=0),
                           lambda kh: jnp.concatenate([vn_ref[:, kh * HEAD_DIM:(kh + 1) * HEAD_DIM], pad], axis=0),
                           s_loc < t_loc, jnp.zeros((n_rows, 1), F32), jnp.zeros((n_rows, HEAD_DIM), F32))
        car_ref[...] = car
        acc_ref[...] = acc

    car = car_ref[...]
    acc = acc_ref[...]
    for ck_ref, cv_ref in zip(ck_refs, cv_refs):
        car, acc = process(lambda kh: ck_ref[0, pl.ds(kh, PAGE_SIZE, stride=KV_HEADS), :],
                           lambda kh: cv_ref[0, pl.ds(kh, PAGE_SIZE, stride=KV_HEADS), :], None, car, acc)
    car_ref[...] = car
    acc_ref[...] = acc

    @pl.when(p == n_steps - 1)
    def _():
        pad = jnp.zeros((S_PAD - QROWS, HEAD_DIM), F32)
        for hq in range(B_HEADS):
            full = jnp.concatenate([acc[hq * QROWS:(hq + 1) * QROWS, :], pad], axis=0)
            o_ref[:, hq * HEAD_DIM:(hq + 1) * HEAD_DIM] = full.astype(o_ref.dtype)


def attn_sample(q_all, kv_all, cache_k, cache_v, page_table, b_logit, o_prev):
    n_seq, n_pages = page_table.shape
    n_steps = n_pages // PAGES_PER_STEP
    rb0 = MP // S_PAD
    kvw = KV_HEADS * HEAD_DIM
    n_phys = cache_k.shape[0]
    ck = cache_k.reshape(n_phys, PAGE_SIZE * KV_HEADS, HEAD_DIM)
    cv = cache_v.reshape(n_phys, PAGE_SIZE * KV_HEADS, HEAD_DIM)
    bias = jnp.broadcast_to(b_logit.astype(F32)[:, None, None], (B_HEADS, QROWS, 1)).reshape(B_HEADS * QROWS, 1)

    def page(u):
        return lambda s, p, pt: (pt[s * n_pages + n_pages - 1 - (p * PAGES_PER_STEP + u)], 0, 0)

    page_specs = [pl.BlockSpec((1, PAGE_SIZE * KV_HEADS, HEAD_DIM), page(u)) for u in range(PAGES_PER_STEP)]
    grid_spec = pltpu.PrefetchScalarGridSpec(
        num_scalar_prefetch=1, grid=(n_seq, n_steps),
        in_specs=[pl.BlockSpec((S_PAD, B_HEADS * HEAD_DIM), lambda s, p, pt: (rb0 + s, 0)),
                  pl.BlockSpec((S_PAD, kvw), lambda s, p, pt: (rb0 + s, 0)),
                  pl.BlockSpec((S_PAD, kvw), lambda s, p, pt: (rb0 + s, 1))] + page_specs + page_specs + [
                  pl.BlockSpec((B_HEADS * QROWS, 1), lambda s, p, pt: (0, 0)),
                  pl.BlockSpec(memory_space=pl.ANY)],
        out_specs=pl.BlockSpec((S_PAD, B_HEADS * HEAD_DIM), lambda s, p, pt: (rb0 + s, 0)),
        scratch_shapes=[pltpu.VMEM((B_HEADS * QROWS, HEAD_DIM), F32),
                        pltpu.VMEM((B_HEADS * QROWS, 1), F32),
                        pltpu.VMEM((B_HEADS * QROWS, HEAD_DIM), F32)])
    args = [page_table.reshape(-1), q_all, kv_all, kv_all] + [ck] * PAGES_PER_STEP + [cv] * PAGES_PER_STEP + [bias, o_prev]
    return pl.pallas_call(
        functools.partial(_attn_sample_body, n_steps=n_steps),
        grid_spec=grid_spec,
        out_shape=jax.ShapeDtypeStruct(o_prev.shape, o_prev.dtype),
        input_output_aliases={len(args) - 1: 0},
        compiler_params=_cparams("arbitrary", "arbitrary"), name="attn_sample")(*args)


PACK_CHUNKS = D_MODEL // 2 // LANES
ROW_CHUNKS = D_MODEL // LANES
HI_MASK = 0xFFFF0000


def _router_body(h_ref, g_ref, w_ref, xp_ref, idx_ref, gate_ref):
    x = h_ref[...]
    tr = x.shape[0]
    xn = x * lax.rsqrt(jnp.mean(x * x, axis=-1, keepdims=True) + RMS_EPS) * g_ref[...]
    xb = xn.astype(BF16)
    bits = pltpu.bitcast(xb.astype(F32), jnp.uint32)
    packed = bits[:, :D_MODEL // 2] | (bits[:, D_MODEL // 2:] >> 16)
    for c in range(PACK_CHUNKS):
        xp_ref[pl.ds(c, tr, stride=PACK_CHUNKS), :] = packed[:, c * LANES:(c + 1) * LANES]
    logits = _dot(xb, w_ref[...].astype(BF16))
    lane = lax.broadcasted_iota(jnp.int32, logits.shape, 1)
    lane_f = lane.astype(F32)
    ninf = -jnp.inf

    def first_max(vals):
        mx = jnp.max(vals, axis=1, keepdims=True)
        return mx, jnp.min(jnp.where(vals == mx, lane_f, float(LANES)), axis=1, keepdims=True)

    gl = jnp.where(lane < N_GROUPS, logits, ninf)
    gmax, grp = first_max(gl)
    g_gate = 1.0 / jnp.sum(jnp.exp(gl - gmax), axis=1, keepdims=True)
    e_lane = lane - N_GROUPS
    lane_grp = jnp.right_shift(e_lane, 3).astype(F32)
    in_grp = (e_lane >= 0) & (e_lane < N_EXPERTS) & (lane_grp == grp)
    el = jnp.where(in_grp, logits, ninf)
    t1, i1 = first_max(el)
    t2, i2 = first_max(jnp.where(lane_f == i1, ninf, el))
    e21 = jnp.exp(t2 - t1)
    g1 = g_gate / (1.0 + e21)
    g2 = g_gate * e21 / (1.0 + e21)
    idx_ref[...] = jnp.where(lane == 0, i1, jnp.where(lane == 1, i2, float(N_GROUPS))).astype(jnp.int32) - N_GROUPS
    gate_ref[...] = jnp.where(lane == 0, g1, jnp.where(lane == 1, g2, 0.0))


def router(h, g_norm, w_group, w_router, tr):
    m, d = h.shape
    w = jnp.pad(jnp.concatenate([w_group, w_router], axis=1).astype(F32), ((0, 0), (0, LANES - N_GROUPS - N_EXPERTS)))
    small = pl.BlockSpec((tr, LANES), lambda i: (i, 0))
    xp, idx, gate = pl.pallas_call(
        _router_body, grid=(m // tr,),
        in_specs=[pl.BlockSpec((tr, d), lambda i: (i, 0)), pl.BlockSpec((1, d), lambda i: (0, 0)),
                  pl.BlockSpec((d, LANES), lambda i: (0, 0))],
        out_specs=[pl.BlockSpec((tr * PACK_CHUNKS, LANES), lambda i: (i, 0)), small, small],
        out_shape=[jax.ShapeDtypeStruct((m * PACK_CHUNKS, LANES), jnp.uint32),
                   jax.ShapeDtypeStruct((m, LANES), jnp.int32), jax.ShapeDtypeStruct((m, LANES), F32)],
        compiler_params=_cparams("arbitrary"), name="router")(h, g_norm.reshape(1, d), w)
    return xp, idx[:, :2], gate[:, :2]


BM = 1024
SUB = 256


def _moe_plan(idx, gates):
    m = idx.shape[0]
    flat_e = idx.reshape(-1)
    onehot = (flat_e[:, None] == jnp.arange(N_EXPERTS, dtype=jnp.int32)[None, :]).astype(jnp.int32)
    csum = jnp.cumsum(onehot, axis=0)
    rank = jnp.take_along_axis(csum, flat_e[:, None], axis=1)[:, 0] - 1
    counts = csum[-1]
    nblk = (counts + BM - 1) // BM
    ends = jnp.cumsum(nblk)
    bstart = ends - nblk
    dest = (bstart[flat_e] * BM + rank).astype(jnp.int32)
    n_blocks = N_EXPERTS + (2 * m) // BM
    n_used = ends[-1]
    used = jnp.arange(n_blocks) < n_used
    blk = jnp.minimum(jnp.arange(n_blocks, dtype=jnp.int32), n_used - 1)
    be = jnp.minimum(jnp.searchsorted(ends, blk, side="right"), N_EXPERTS - 1).astype(jnp.int32)
    cnt = jnp.clip(counts[be] - (blk - bstart[be]) * BM, 0, BM)
    cnt = jnp.where(used, cnt, 0).astype(jnp.int32)
    n_rows = n_blocks * BM
    tok = jnp.arange(2 * m, dtype=jnp.int32) // 2
    row_src = jnp.zeros((n_rows,), jnp.int32).at[dest].set(tok)
    row_gate = jnp.zeros((n_rows,), F32).at[dest].set(gates.reshape(-1))
    return dict(be=be, bx=blk.astype(jnp.int32), cnt=cnt, row_src=row_src, row_gate=row_gate, dest=dest,
                n_blocks=n_blocks, n_rows=n_rows)


def _gather_rows(idx_of, n, src_ref, dst_of, sem, rs):
    def start(r, carry):
        src_row = pl.multiple_of(idx_of(r) * rs, rs)
        pltpu.make_async_copy(src_ref.at[pl.ds(src_row, rs)], dst_of(r), sem).start()
        return carry

    lax.fori_loop(0, n, start, 0)


def _ffn_in_body(be_ref, bx_ref, cnt_ref, src_ref, xp_ref, wg_ref, wu_ref, o_ref, xg_ref, xs_ref, sem):
    del be_ref, bx_ref
    b = pl.program_id(0)
    cnt = cnt_ref[b]
    n_sub = (cnt + SUB - 1) // SUB
    sub_words = SUB * PACK_CHUNKS

    @pl.when((pl.program_id(1) == 0) & (cnt > 0))
    def _():
        _gather_rows(lambda r: src_ref[0, 0, r], n_sub * SUB, xp_ref,
                     lambda r: xg_ref.at[pl.ds(pl.multiple_of(r * PACK_CHUNKS, PACK_CHUNKS), PACK_CHUNKS)],
                     sem, PACK_CHUNKS)
        for sub in range(BM // SUB):
            @pl.when(sub < n_sub)
            def _():
                rows = pl.ds(sub * sub_words, sub_words)
                pltpu.make_async_copy(xp_ref.at[rows], xg_ref.at[rows], sem).wait()
        half = D_MODEL // 2
        for sub in range(BM // SUB):
            @pl.when(sub < n_sub)
            def _():
                rows = slice(sub * SUB, (sub + 1) * SUB)
                for c in range(PACK_CHUNKS):
                    w = xg_ref[pl.ds(sub * sub_words + c, SUB, stride=PACK_CHUNKS), :]
                    xs_ref[rows, c * LANES:(c + 1) * LANES] = pltpu.bitcast(w & jnp.uint32(HI_MASK), F32).astype(BF16)
                    xs_ref[rows, half + c * LANES:half + (c + 1) * LANES] = pltpu.bitcast(w << 16, F32).astype(BF16)

    @pl.when(cnt > 0)
    def _():
        wg = wg_ref[0, 0].astype(BF16)
        wu = wu_ref[0, 0].astype(BF16)
        for sub in range(BM // SUB):
            @pl.when(sub * SUB < cnt)
            def _():
                x = xs_ref[sub * SUB:(sub + 1) * SUB, :]
                g = _dot(x, wg)
                u = _dot(x, wu)
                o_ref[sub * SUB:(sub + 1) * SUB, :] = (g * (1.0 / (1.0 + jnp.exp(-g))) * u).astype(o_ref.dtype)


def _ffn_out_body(be_ref, bx_ref, cnt_ref, h_ref, w_ref, gate_ref, o_ref, wb_ref):
    del bx_ref
    b = pl.program_id(1)
    cnt = cnt_ref[b]
    new_w = (b == 0) | (be_ref[b] != be_ref[jnp.maximum(b - 1, 0)])

    @pl.when(new_w)
    def _():
        wb_ref[...] = w_ref[0, 0].astype(BF16)

    n_chunks = o_ref.shape[1]
    o2 = o_ref.reshape(BM * n_chunks, LANES)
    for sub in range(BM // SUB):
        @pl.when(sub * SUB < cnt)
        def _():
            y = _dot(h_ref[sub * SUB:(sub + 1) * SUB, :], wb_ref[...]) * gate_ref[sub * SUB:(sub + 1) * SUB, 0:1]
            for c in range(n_chunks):
                o2[pl.ds(sub * SUB * n_chunks + c, SUB, stride=n_chunks), :] = y[:, c * LANES:(c + 1) * LANES]


def expert_ffn(xp, plan, w_in, w_out, layer, tn_in=256, tn_out=1024):
    n_rows = plan["n_rows"]
    nb = plan["n_blocks"]
    d = D_MODEL
    nci = D_EXPERT // tn_in

    def col(c, b, cnt):
        return jnp.where(cnt[b] > 0, c, nci - 1)

    h_mid = pl.pallas_call(
        _ffn_in_body,
        grid_spec=pltpu.PrefetchScalarGridSpec(
            num_scalar_prefetch=3, grid=(nb, nci),
            in_specs=[pl.BlockSpec((1, 1, BM), lambda b, c, be, bx, cnt: (bx[b], 0, 0), memory_space=pltpu.SMEM),
                      pl.BlockSpec(memory_space=pl.ANY),
                      pl.BlockSpec((1, 1, d, tn_in), lambda b, c, be, bx, cnt: (layer, be[b], 0, col(c, b, cnt))),
                      pl.BlockSpec((1, 1, d, tn_in), lambda b, c, be, bx, cnt: (layer, be[b], 0, nci + col(c, b, cnt)))],
            out_specs=pl.BlockSpec((BM, tn_in), lambda b, c, be, bx, cnt: (bx[b], col(c, b, cnt))),
            scratch_shapes=[pltpu.VMEM((BM * PACK_CHUNKS, LANES), jnp.uint32), pltpu.VMEM((BM, d), BF16),
                            pltpu.SemaphoreType.DMA(())]),
        out_shape=jax.ShapeDtypeStruct((n_rows, D_EXPERT), BF16),
        compiler_params=_cparams("arbitrary", "arbitrary"), name="ffn_in")(
            plan["be"], plan["bx"], plan["cnt"], plan["row_src"].reshape(nb, 1, BM), xp, w_in, w_in)
    gate = jnp.broadcast_to(plan["row_gate"][:, None], (n_rows, LANES))
    nco = d // tn_out
    oc = tn_out // LANES
    return pl.pallas_call(
        _ffn_out_body,
        grid_spec=pltpu.PrefetchScalarGridSpec(
            num_scalar_prefetch=3, grid=(nco, nb),
            in_specs=[pl.BlockSpec((BM, D_EXPERT), lambda c, b, be, bx, cnt: (bx[b], 0)),
                      pl.BlockSpec((1, 1, D_EXPERT, tn_out), lambda c, b, be, bx, cnt: (layer, be[b], 0, c)),
                      pl.BlockSpec((BM, LANES), lambda c, b, be, bx, cnt: (bx[b], 0))],
            out_specs=pl.BlockSpec((BM, oc, LANES), lambda c, b, be, bx, cnt: (bx[b], c, 0)),
            scratch_shapes=[pltpu.VMEM((D_EXPERT, tn_out), BF16)]),
        out_shape=jax.ShapeDtypeStruct((n_rows, ROW_CHUNKS, LANES), F32),
        compiler_params=_cparams("arbitrary", "arbitrary"), name="ffn_out")(
            plan["be"], plan["bx"], plan["cnt"], h_mid, w_out, gate)


def _combine_body(dest_ref, h_ref, ys_ref, *rest, n_norm):
    tr = h_ref.shape[0]
    if n_norm:
        g_ref, o_ref, *norm_refs, y_ref, sem = rest
    else:
        o_ref, y_ref, sem = rest
    tile_rows = tr * ROW_CHUNKS
    _gather_rows(lambda e: dest_ref[0, 0, e], 2 * tr, ys_ref,
                 lambda e: y_ref.at[e & 1, pl.ds(pl.multiple_of((e >> 1) * ROW_CHUNKS, ROW_CHUNKS), ROW_CHUNKS)],
                 sem, ROW_CHUNKS)
    for slot in range(2):
        pltpu.make_async_copy(ys_ref.at[pl.ds(0, tile_rows)], y_ref.at[slot], sem).wait()
    ss = jnp.zeros((tr, 1), F32)
    for c in range(ROW_CHUNKS):
        cols = slice(c * LANES, (c + 1) * LANES)
        hn = h_ref[:, cols] + y_ref[0, pl.ds(c, tr, stride=ROW_CHUNKS), :] + y_ref[1, pl.ds(c, tr, stride=ROW_CHUNKS), :]
        o_ref[:, cols] = hn
        ss = ss + jnp.sum(hn * hn, axis=1, keepdims=True)
    if n_norm:
        r = lax.rsqrt(ss * (1.0 / D_MODEL) + RMS_EPS)
        for c in range(ROW_CHUNKS):
            cols = slice(c * LANES, (c + 1) * LANES)
            xr = o_ref[:, cols] * r
            for j, n_ref in enumerate(norm_refs):
                n_ref[:, cols] = (xr * g_ref[j:j + 1, cols]).astype(n_ref.dtype)


def combine(h, y_sorted, dest, gains, tr):
    m, d = h.shape
    n_norm = 0 if gains is None else gains.shape[0]
    row = pl.BlockSpec((tr, d), lambda i: (i, 0))
    in_specs = [pl.BlockSpec((1, 1, 2 * tr), lambda i: (i, 0, 0), memory_space=pltpu.SMEM), row,
                pl.BlockSpec(memory_space=pl.ANY)]
    args = [dest.reshape(m // tr, 1, 2 * tr), h, y_sorted]
    if n_norm:
        in_specs.append(pl.BlockSpec((n_norm, d), lambda i: (0, 0)))
        args.append(gains)
    return pl.pallas_call(
        functools.partial(_combine_body, n_norm=n_norm), grid=(m // tr,),
        in_specs=in_specs, out_specs=[row] * (1 + n_norm),
        out_shape=[jax.ShapeDtypeStruct((m, d), F32)] + [jax.ShapeDtypeStruct((m, d), BF16)] * n_norm,
        scratch_shapes=[pltpu.VMEM((2, tr * ROW_CHUNKS, LANES), F32), pltpu.SemaphoreType.DMA(())],
        compiler_params=_cparams("arbitrary"), name="combine")(*args)


def hier_moe_layer(h, g_norm, w_group, w_router, w_in, w_out, layer, next_gains, tr):
    xp, idx, gates = router(h, g_norm, w_group, w_router, tr)
    plan = _moe_plan(idx, gates)
    y_sorted = expert_ffn(xp, plan, w_in, w_out, layer)
    return combine(h, y_sorted.reshape(plan["n_rows"] * ROW_CHUNKS, LANES), plan["dest"], next_gains, tr)


TM = 640
TR = 128


def kernel(x_prompt, x_sample, state_C, state_n, state_m, cache_k, cache_v, page_table, norm_mix, w_in_a, b_gate_a,
           g_hnorm_a, w_out_a, g_kv, w_kv, g_knorm, w_q_b, g_qnorm_b, b_logit_b, w_out_b, norm_ffn, w_group,
           w_router, w_moe_in, w_moe_out):
    xs = jnp.pad(x_sample, ((0, 0), (0, S_PAD - DEC_SEQ), (0, 0))).reshape(MS, D_MODEL)
    h0 = jnp.concatenate([x_prompt.reshape(MP, D_MODEL), xs], axis=0)

    (xn0,) = rms_norm(h0, norm_mix[0:1], [BF16], TR)
    p_main = matmul(xn0, w_in_a[0], A_MAIN, F32, TM, 512)
    gates = mlstm_gates(xn0, w_in_a[0][:, A_MAIN:], b_gate_a[0], TM)
    hg, pc, pn, pm = mlstm(p_main, gates, g_hnorm_a[0], None, 0, BATCH, SEQ, SEQ, 256, 256, MLSTM_SUB)
    hg, sc, sn, sm = mlstm(p_main, gates, g_hnorm_a[0], hg, MP, DEC_BATCH, S_PAD, DEC_SEQ, S_PAD, LANES, LANES,
                           init=(state_C[0], state_n[0], state_m[0]))
    h1 = matmul(hg, w_out_a[0], D_MODEL, F32, TM, 512, mode="resid", resid=h0)
    h1, xn1, xkv = hier_moe_layer(h1, norm_ffn[0], w_group[0], w_router[0], w_moe_in, w_moe_out, 0,
                                  jnp.stack([norm_mix[1], g_kv]), TR)

    kv_all = matmul(xkv, w_kv, 2 * KV_HEADS * HEAD_DIM, F32, TM, 512, mode="headnorm", gain=g_knorm,
                    n_norm_tiles=KV_HEADS * HEAD_DIM // 512)
    q_all = matmul(xn1, w_q_b[0], B_HEADS * HEAD_DIM, BF16, TM, 512, mode="headnorm", gain=g_qnorm_b[0],
                   n_norm_tiles=B_HEADS * HEAD_DIM // 512)
    o_all = attn_prompt(q_all, kv_all, b_logit_b[0])
    o_all = attn_sample(q_all, kv_all, cache_k, cache_v, page_table, b_logit_b[0], o_all)
    h2 = matmul(o_all, w_out_b[0], D_MODEL, F32, TM, 512, mode="resid", resid=h1)
    (h3,) = hier_moe_layer(h2, norm_ffn[1], w_group[1], w_router[1], w_moe_in, w_moe_out, 1, None, TR)

    def sample_rows(a):
        return a[MP:].reshape((DEC_BATCH, S_PAD) + a.shape[1:])[:, :DEC_SEQ]

    kvw = KV_HEADS * HEAD_DIM
    y_prompt = h3[:MP].reshape(BATCH, SEQ, D_MODEL)
    y_sample = sample_rows(h3)
    prompt_k = kv_all[:MP, :kvw].reshape(BATCH, SEQ, KV_HEADS, HEAD_DIM)
    prompt_v = kv_all[:MP, kvw:].reshape(BATCH, SEQ, KV_HEADS, HEAD_DIM)
    sample_k = sample_rows(kv_all[:, :kvw]).reshape(DEC_BATCH, DEC_SEQ, KV_HEADS, HEAD_DIM)
    sample_v = sample_rows(kv_all[:, kvw:]).reshape(DEC_BATCH, DEC_SEQ, KV_HEADS, HEAD_DIM)
    return (y_prompt, y_sample, pc[None], pn[None], pm[None], prompt_k, prompt_v,
            sc[None], sn[None], sm[None], sample_k, sample_v)
```

```python
import functools

import jax
import jax.numpy as jnp
from jax import lax
from jax.experimental import pallas as pl
from jax.experimental.pallas import tpu as pltpu

F32 = jnp.float32
BF16 = jnp.bfloat16

D_MODEL = 4096
BATCH = 4
SEQ = 2048
DEC_BATCH = 8
DEC_SEQ = 4
PAGE_SIZE = 128
A_HEADS = 8
A_QK_DIM = 256
A_V_DIM = 512
A_QK = A_HEADS * A_QK_DIM
A_VD = A_HEADS * A_V_DIM
A_MAIN = 2 * A_QK + A_VD + D_MODEL
HEAD_DIM = 128
B_HEADS = 32
KV_HEADS = 8
Q_PER_KV = B_HEADS // KV_HEADS
N_GROUPS = 4
EXPERTS_PER_GROUP = 8
N_EXPERTS = N_GROUPS * EXPERTS_PER_GROUP
D_EXPERT = 1024
RMS_EPS = 1e-6

LANES = 128
MP = BATCH * SEQ
S_PAD = 16
MS = DEC_BATCH * S_PAD
M_ALL = MP + MS
NEG_BIG = -1e30
VMEM_LIMIT = 56 * 1024 * 1024

HIGHEST = lax.Precision.HIGHEST


def _cparams(*sem):
    return pltpu.CompilerParams(dimension_semantics=sem, vmem_limit_bytes=VMEM_LIMIT)


def _dot(a, b, precision=None):
    return jnp.dot(a, b, preferred_element_type=F32, precision=precision)


def _dot_nt(a, b):
    return lax.dot_general(a, b, (((1,), (1,)), ((), ())), preferred_element_type=F32)


def _log_sigmoid(x):
    return jnp.minimum(x, 0.0) - jnp.log1p(jnp.exp(-jnp.abs(x)))


def _stream_specs(src, tr):
    row = pl.BlockSpec((tr, D_MODEL), lambda i: (i, 0))
    if len(src) == 1:
        return [row]
    assert tr == MS and MP % tr == 0
    last_prompt = MP // tr - 1
    return [pl.BlockSpec((tr, D_MODEL), lambda i: (jnp.minimum(i, last_prompt), 0)),
            pl.BlockSpec((tr, D_MODEL), lambda i: (0, 0))] + [row] * (len(src) - 2)


def _stream_tile(refs):
    if len(refs) == 1:
        return refs[0][...]
    tr = refs[0].shape[0]
    base = jnp.where(pl.program_id(0) < MP // tr, refs[0][...], refs[1][...])
    return base + refs[2][...] if len(refs) == 3 else base


def _rms_body(*refs, n_src):
    g_ref, *out_refs = refs[n_src:]
    x = _stream_tile(refs[:n_src])
    xr = x * lax.rsqrt(jnp.mean(x * x, axis=-1, keepdims=True) + RMS_EPS)
    for j, o_ref in enumerate(out_refs):
        o_ref[...] = (xr * g_ref[j:j + 1, :]).astype(o_ref.dtype)


def rms_norm(src, gains, out_dtypes, tr):
    n = len(out_dtypes)
    row = pl.BlockSpec((tr, D_MODEL), lambda i: (i, 0))
    return pl.pallas_call(
        functools.partial(_rms_body, n_src=len(src)), grid=(M_ALL // tr,),
        in_specs=_stream_specs(src, tr) + [pl.BlockSpec((n, D_MODEL), lambda i: (0, 0))],
        out_specs=[row] * n,
        out_shape=[jax.ShapeDtypeStruct((M_ALL, D_MODEL), dt) for dt in out_dtypes],
        compiler_params=_cparams("arbitrary"), name="rms_norm")(*src, gains)


def _mm_body(x_ref, w_ref, *rest, mode, n_norm_tiles):
    if mode == "resid":
        r_ref, o_ref, wb_ref = rest
    elif mode == "headnorm":
        g_ref, o_ref, wb_ref = rest
    else:
        o_ref, wb_ref = rest
    j = pl.program_id(0)

    @pl.when(pl.program_id(1) == 0)
    def _():
        wb_ref[...] = w_ref[...].astype(BF16)

    acc = _dot(x_ref[...], wb_ref[...])
    if mode == "plain":
        o_ref[...] = acc.astype(o_ref.dtype)
    elif mode == "resid":
        o_ref[...] = r_ref[...] + acc
    else:
        @pl.when(j < n_norm_tiles)
        def _():
            for c in range(acc.shape[1] // HEAD_DIM):
                a = acc[:, c * HEAD_DIM:(c + 1) * HEAD_DIM]
                y = a * lax.rsqrt(jnp.mean(a * a, axis=-1, keepdims=True) + RMS_EPS) * g_ref[...]
                o_ref[:, c * HEAD_DIM:(c + 1) * HEAD_DIM] = y.astype(o_ref.dtype)

        @pl.when(j >= n_norm_tiles)
        def _():
            o_ref[...] = acc.astype(o_ref.dtype)


def matmul(x, w, n_out, out_dtype, tm, tn, mode="plain", resid=None, gain=None, n_norm_tiles=0):
    m, k = x.shape
    in_specs = [pl.BlockSpec((tm, k), lambda j, i: (i, 0)), pl.BlockSpec((k, tn), lambda j, i: (0, j))]
    args = [x, w]
    if mode == "resid":
        in_specs.append(pl.BlockSpec((tm, tn), lambda j, i: (i, j)))
        args.append(resid)
    elif mode == "headnorm":
        in_specs.append(pl.BlockSpec((1, HEAD_DIM), lambda j, i: (0, 0)))
        args.append(gain.reshape(1, HEAD_DIM))
    return pl.pallas_call(
        functools.partial(_mm_body, mode=mode, n_norm_tiles=n_norm_tiles),
        grid=(n_out // tn, m // tm),
        in_specs=in_specs,
        out_specs=pl.BlockSpec((tm, tn), lambda j, i: (i, j)),
        out_shape=jax.ShapeDtypeStruct((m, n_out), out_dtype),
        scratch_shapes=[pltpu.VMEM((k, tn), BF16)],
        compiler_params=_cparams("arbitrary", "arbitrary"), name="matmul_" + mode)(*args)


def _gates_body(x_ref, w_ref, b_ref, o_ref):
    pre = _dot(x_ref[...], w_ref[...].astype(BF16)) + b_ref[...]
    lane = lax.broadcasted_iota(jnp.int32, pre.shape, 1)
    o_ref[...] = jnp.where(lane < A_HEADS, pre, jnp.where(lane < 2 * A_HEADS, _log_sigmoid(pre), 0.0))


def mlstm_gates(x, w_gate, b_gate, tm):
    m, k = x.shape
    w = jnp.pad(w_gate, ((0, 0), (0, LANES - w_gate.shape[1])))
    b = jnp.pad(b_gate.astype(F32), (0, LANES - b_gate.shape[0])).reshape(1, LANES)
    return pl.pallas_call(
        _gates_body, grid=(m // tm,),
        in_specs=[pl.BlockSpec((tm, k), lambda i: (i, 0)), pl.BlockSpec((k, LANES), lambda i: (0, 0)),
                  pl.BlockSpec((1, LANES), lambda i: (0, 0))],
        out_specs=pl.BlockSpec((tm, LANES), lambda i: (i, 0)),
        out_shape=jax.ShapeDtypeStruct((m, LANES), F32),
        compiler_params=_cparams("arbitrary"), name="mlstm_gates")(x, w, b)


MLSTM_SUB = 64
def _mlstm_body(*refs, chunk, lb, sub, t_valid, t_total, has_init):
    if has_init:
        q_ref, k_ref, v_ref, o_ref, ga_ref, gh_ref, c0_ref, n0_ref, m0_ref, hg_ref, c_ref, n_ref, m_ref = refs
    else:
        q_ref, k_ref, v_ref, o_ref, ga_ref, gh_ref, hg_ref, c_ref, n_ref, m_ref = refs
    h = pl.program_id(1)
    c = pl.program_id(2)

    @pl.when(c == 0)
    def _():
        if has_init:
            c_ref[...] = c0_ref[...]
            n_ref[...] = n0_ref[...]
            m_ref[...] = m0_ref[...]
        else:
            c_ref[...] = jnp.zeros_like(c_ref)
            n_ref[...] = jnp.zeros_like(n_ref)
            m_ref[...] = jnp.zeros_like(m_ref)

    def load(ref):
        x = ref[...]
        if lb == chunk:
            return x
        return jnp.concatenate([x, jnp.zeros((chunk - lb, x.shape[1]), x.dtype)], axis=0)

    q = load(q_ref).astype(BF16)
    k = load(k_ref).astype(BF16)
    v = load(v_ref).astype(BF16)
    ga = load(ga_ref)
    c_state = c_ref[0, 0]
    n_state = n_ref[0, 0]
    m_state = m_ref[0, 0][:, 0:1]
    n_sub = chunk // sub
    sub_shift = sub.bit_length() - 1

    lane = lax.broadcasted_iota(jnp.int32, (chunk, LANES), 1)
    i_col = jnp.sum(jnp.where(lane == h, ga, 0.0), axis=1, keepdims=True)
    f_col = jnp.sum(jnp.where(lane == h + A_HEADS, ga, 0.0), axis=1, keepdims=True)
    if t_valid < t_total:
        t_idx = lax.broadcasted_iota(jnp.int32, (chunk, 1), 0) + c * chunk
        i_col = jnp.where(t_idx < t_valid, i_col, NEG_BIG)
        f_col = jnp.where(t_idx < t_valid, f_col, 0.0)

    row_i = lax.broadcasted_iota(jnp.int32, (chunk, chunk), 0)
    col_i = lax.broadcasted_iota(jnp.int32, (chunk, chunk), 1)
    causal = (col_i <= row_i) & (jnp.right_shift(row_i, sub_shift) == jnp.right_shift(col_i, sub_shift))
    g2 = jnp.where(lane == 0, i_col, jnp.where(lane == 1, f_col, 0.0))
    cum = _dot(causal.astype(F32), g2, precision=HIGHEST)
    g3 = jnp.where(lane == 1, cum, g2)
    g3t = g3.T
    i_row = g3t[0:1, :]
    b_row = g3t[1:2, :]
    b_col = g3[:, 1:2]
    sub_of_row = jnp.right_shift(lax.broadcasted_iota(jnp.int32, (1, chunk), 1), sub_shift)
    sub_of_col = jnp.right_shift(lax.broadcasted_iota(jnp.int32, (chunk, 1), 0), sub_shift)

    m_at = [m_state]
    b_last = []
    for j in range(n_sub):
        bl = b_col[(j + 1) * sub - 1:(j + 1) * sub, :]
        g = bl - b_col[j * sub:(j + 1) * sub, :] + i_col[j * sub:(j + 1) * sub, :]
        b_last.append(bl)
        m_at.append(jnp.maximum(bl + m_at[j], jnp.max(g, axis=0, keepdims=True)))
    m_col = jnp.concatenate([jnp.broadcast_to(m_at[j], (sub, 1)) for j in range(n_sub)], axis=0)

    dmat = jnp.where(causal, b_col - b_row + i_row, NEG_BIG)
    inter = b_col + m_col
    m_t = jnp.maximum(inter, jnp.max(dmat, axis=1, keepdims=True))
    w_intra = jnp.exp(dmat - m_t)
    w_inter = jnp.exp(inter - m_t)
    scale = A_QK_DIM ** -0.5
    s = _dot_nt(q, k) * scale * w_intra
    intra = _dot(s.astype(BF16), v)
    s_sum = jnp.sum(s, axis=1, keepdims=True)

    kf = k.astype(F32)
    kt = kf.T
    qc_parts = []
    qn_parts = []
    c_cur = c_state
    n_cur = n_state
    for j in range(n_sub):
        qj = q[j * sub:(j + 1) * sub]
        qc_parts.append(_dot(qj, c_cur.astype(BF16)))
        qn_parts.append(jnp.sum(qj.astype(F32) * n_cur.astype(BF16).astype(F32), axis=1, keepdims=True))
        m_new = m_at[j + 1]
        wk_row = jnp.where(sub_of_row == j, jnp.exp(b_last[j] - b_row + i_row - m_new), 0.0)
        wk_col = jnp.where(sub_of_col == j, jnp.exp(b_last[j] - b_col + i_col - m_new), 0.0)
        decay = jnp.exp(b_last[j] + m_at[j] - m_new)
        c_cur = decay * c_cur + _dot((kt * wk_row).astype(BF16), v)
        n_cur = decay * n_cur + jnp.sum(kf * wk_col.astype(BF16).astype(F32), axis=0, keepdims=True)
    num = w_inter * (jnp.concatenate(qc_parts, axis=0) * scale) + intra
    den = w_inter * (jnp.concatenate(qn_parts, axis=0) * scale) + s_sum
    hh = num / jnp.maximum(jnp.abs(den), jnp.exp(-m_t))

    hn = hh * lax.rsqrt(jnp.mean(hh * hh, axis=1, keepdims=True) + RMS_EPS) * gh_ref[0]
    og = o_ref[...].astype(F32)
    out = hn[0:lb] * (1.0 / (1.0 + jnp.exp(-og)))
    hg_ref[...] = out.astype(hg_ref.dtype)

    c_ref[0, 0] = c_cur
    n_ref[0, 0] = n_cur
    m_ref[0, 0] = jnp.broadcast_to(m_at[n_sub], (1, LANES))


def mlstm(p_main, gates, g_hnorm, hg_prev, row0, n_seq, t_total, t_valid, lb, chunk, sub, init=None):
    m = p_main.shape[0]
    nc = t_total // lb
    rb0 = row0 // lb

    def rows(b, h, c):
        return rb0 + b * nc + c

    qk_blocks = A_QK // A_QK_DIM
    in_specs = [
        pl.BlockSpec((lb, A_QK_DIM), lambda b, h, c: (rows(b, h, c), h)),
        pl.BlockSpec((lb, A_QK_DIM), lambda b, h, c: (rows(b, h, c), qk_blocks + h)),
        pl.BlockSpec((lb, A_V_DIM), lambda b, h, c: (rows(b, h, c), 2 * A_QK // A_V_DIM + h)),
        pl.BlockSpec((lb, A_V_DIM), lambda b, h, c: (rows(b, h, c), (2 * A_QK + A_VD) // A_V_DIM + h)),
        pl.BlockSpec((lb, LANES), lambda b, h, c: (rows(b, h, c), 0)),
        pl.BlockSpec((1, 1, A_V_DIM), lambda b, h, c: (h, 0, 0)),
    ]
    args = [p_main, p_main, p_main, p_main, gates, g_hnorm.reshape(A_HEADS, 1, A_V_DIM)]
    state_specs = [
        pl.BlockSpec((1, 1, A_QK_DIM, A_V_DIM), lambda b, h, c: (b, h, 0, 0)),
        pl.BlockSpec((1, 1, 1, A_QK_DIM), lambda b, h, c: (b, h, 0, 0)),
        pl.BlockSpec((1, 1, 1, LANES), lambda b, h, c: (b, h, 0, 0)),
    ]
    has_init = init is not None
    if has_init:
        c0, n0, m0 = init
        in_specs += state_specs
        args += [c0.astype(F32), n0.astype(F32).reshape(n_seq, A_HEADS, 1, A_QK_DIM),
                 jnp.broadcast_to(m0.astype(F32)[:, :, None, None], (n_seq, A_HEADS, 1, LANES))]
    aliases = {}
    if hg_prev is not None:
        in_specs.append(pl.BlockSpec(memory_space=pl.ANY))
        args.append(hg_prev)
        aliases = {len(args) - 1: 0}

    def body(*refs):
        if hg_prev is not None:
            n_in = len(args)
            refs = refs[:n_in - 1] + refs[n_in:]
        _mlstm_body(*refs, chunk=chunk, lb=lb, sub=sub, t_valid=t_valid, t_total=t_total, has_init=has_init)

    hg, c_out, n_out, m_out = pl.pallas_call(
        body, grid=(n_seq, A_HEADS, nc),
        in_specs=in_specs,
        out_specs=[pl.BlockSpec((lb, A_V_DIM), lambda b, h, c: (rows(b, h, c), h))] + state_specs,
        out_shape=[jax.ShapeDtypeStruct((m, A_VD), BF16),
                   jax.ShapeDtypeStruct((n_seq, A_HEADS, A_QK_DIM, A_V_DIM), F32),
                   jax.ShapeDtypeStruct((n_seq, A_HEADS, 1, A_QK_DIM), F32),
                   jax.ShapeDtypeStruct((n_seq, A_HEADS, 1, LANES), F32)],
        input_output_aliases=aliases,
        compiler_params=_cparams("arbitrary", "arbitrary", "arbitrary"), name="mlstm")(*args)
    return hg, c_out, n_out.reshape(n_seq, A_HEADS, A_QK_DIM), m_out[:, :, 0, 0]


def _sb_block(z, tri, carry, mask):
    l1p = jnp.log(1.0 + jnp.exp(-jnp.abs(z)))
    ls = jnp.minimum(z, 0.0) - l1p
    u = ls - z
    if mask is not None:
        u = jnp.where(mask, u, 0.0)
    u_hi = u.astype(BF16)
    u_lo = (u - u_hi.astype(F32)).astype(BF16)
    within = _dot(u_hi, tri) + _dot(u_lo, tri)
    a = jnp.exp(ls + within + carry)
    if mask is not None:
        a = jnp.where(mask, a, 0.0)
    return a, carry + within[:, 0:1] + u[:, 0:1]


def _strict_tri(tk):
    r = lax.broadcasted_iota(jnp.int32, (tk, tk), 0)
    c = lax.broadcasted_iota(jnp.int32, (tk, tk), 1)
    return (r > c).astype(BF16)


def _attn_prompt_body(q_ref, k_ref, v_ref, b_ref, o_ref, *, tq, tk):
    qi = pl.program_id(2)
    scale = HEAD_DIM ** -0.5
    tri = _strict_tri(tk)
    t_loc = lax.broadcasted_iota(jnp.int32, (tq, tk), 0)
    s_loc = lax.broadcasted_iota(jnp.int32, (tq, tk), 1)
    nkb = tq // tk
    rows = Q_PER_KV * tq
    qs = jnp.concatenate([q_ref[:, g * HEAD_DIM:(g + 1) * HEAD_DIM] for g in range(Q_PER_KV)], axis=0)
    bias = jnp.concatenate([jnp.broadcast_to(b_ref[0, :, g:g + 1], (tq, 1)) for g in range(Q_PER_KV)], axis=0)

    def visit(j, state, diag_off):
        carry, acc = state
        start = pl.multiple_of(j * tk, tk)
        kb = k_ref[pl.ds(start, tk), :].astype(BF16)
        vb = v_ref[pl.ds(start, tk), :].astype(BF16)
        z = _dot_nt(qs, kb) * scale + bias
        mask = None
        if diag_off is not None:
            mask = jnp.concatenate([(s_loc + diag_off * tk) < t_loc] * Q_PER_KV, axis=0)
        a, carry = _sb_block(z, tri, carry, mask)
        return carry, acc + _dot(a.astype(BF16), vb)

    state = (jnp.zeros((rows, 1), F32), jnp.zeros((rows, HEAD_DIM), F32))
    for d in reversed(range(nkb)):
        state = visit(qi * nkb + d, state, d)
    n_before = qi * nkb
    state = lax.fori_loop(0, n_before, lambda it, st: visit(n_before - 1 - it, st, None), state)
    for g in range(Q_PER_KV):
        o_ref[:, g * HEAD_DIM:(g + 1) * HEAD_DIM] = state[1][g * tq:(g + 1) * tq].astype(o_ref.dtype)


def attn_prompt(q_all, kv_all, b_logit, tq=256, tk=128):
    m = q_all.shape[0]
    nq = SEQ // tq
    gw = Q_PER_KV * HEAD_DIM
    bias = b_logit.astype(F32).reshape(KV_HEADS, 1, Q_PER_KV)
    return pl.pallas_call(
        functools.partial(_attn_prompt_body, tq=tq, tk=tk),
        grid=(BATCH, KV_HEADS, nq),
        in_specs=[pl.BlockSpec((tq, gw), lambda b, h, i: (b * nq + i, h)),
                  pl.BlockSpec((SEQ, HEAD_DIM), lambda b, h, i: (b, h)),
                  pl.BlockSpec((SEQ, HEAD_DIM), lambda b, h, i: (b, KV_HEADS + h)),
                  pl.BlockSpec((1, 1, Q_PER_KV), lambda b, h, i: (h, 0, 0))],
        out_specs=pl.BlockSpec((tq, gw), lambda b, h, i: (b * nq + i, h)),
        out_shape=jax.ShapeDtypeStruct((m, B_HEADS * HEAD_DIM), BF16),
        compiler_params=_cparams("arbitrary", "arbitrary", "arbitrary"), name="attn_prompt")(q_all, kv_all, kv_all, bias)


QROWS = 8
PAGES_PER_STEP = 4


def _attn_sample_body(pt_ref, q_ref, kn_ref, vn_ref, *rest, n_steps):
    ck_refs = rest[:PAGES_PER_STEP]
    cv_refs = rest[PAGES_PER_STEP:2 * PAGES_PER_STEP]
    b_ref, oin_ref, o_ref, qs_ref, car_ref, acc_ref = rest[2 * PAGES_PER_STEP:]
    del pt_ref, oin_ref
    p = pl.program_id(1)
    scale = HEAD_DIM ** -0.5
    rows_h = Q_PER_KV * QROWS
    tri = _strict_tri(PAGE_SIZE)

    def process(get_k, get_v, mask, car, acc):
        qs = qs_ref[...]
        zs = [_dot_nt(qs[kh * rows_h:(kh + 1) * rows_h].astype(BF16), get_k(kh).astype(BF16)) for kh in range(KV_HEADS)]
        z = jnp.concatenate(zs, axis=0) * scale + b_ref[...]
        a, car = _sb_block(z, tri, car, mask)
        ab = a.astype(BF16)
        accs = [acc[kh * rows_h:(kh + 1) * rows_h] + _dot(ab[kh * rows_h:(kh + 1) * rows_h], get_v(kh).astype(BF16))
                for kh in range(KV_HEADS)]
        return car, jnp.concatenate(accs, axis=0)

    @pl.when(p == 0)
    def _():
        for hq in range(B_HEADS):
            qs_ref[hq * QROWS:(hq + 1) * QROWS, :] = q_ref[:, hq * HEAD_DIM:(hq + 1) * HEAD_DIM].astype(F32)[0:QROWS]
        pad = jnp.zeros((PAGE_SIZE - S_PAD, HEAD_DIM), F32)
        n_rows = B_HEADS * QROWS
        t_loc = jnp.bitwise_and(lax.broadcasted_iota(jnp.int32, (n_rows, PAGE_SIZE), 0), QROWS - 1)
        s_loc = lax.broadcasted_iota(jnp.int32, (n_rows, PAGE_SIZE), 1)
        car, acc = process(lambda kh: jnp.concatenate([kn_ref[:, kh * HEAD_DIM:(kh + 1) * HEAD_DIM], pad], axis=0),
                           lambda kh: jnp.concatenate([vn_ref[:, kh * HEAD_DIM:(kh + 1) * HEAD_DIM], pad], axis=0),
                           s_loc < t_loc, jnp.zeros((n_rows, 1), F32), jnp.zeros((n_rows, HEAD_DIM), F32))
        car_ref[...] = car
        acc_ref[...] = acc

    car = car_ref[...]
    acc = acc_ref[...]
    for ck_ref, cv_ref in zip(ck_refs, cv_refs):
        car, acc = process(lambda kh: ck_ref[0, pl.ds(kh, PAGE_SIZE, stride=KV_HEADS), :],
                           lambda kh: cv_ref[0, pl.ds(kh, PAGE_SIZE, stride=KV_HEADS), :], None, car, acc)
    car_ref[...] = car
    acc_ref[...] = acc

    @pl.when(p == n_steps - 1)
    def _():
        pad = jnp.zeros((S_PAD - QROWS, HEAD_DIM), F32)
        for hq in range(B_HEADS):
            full = jnp.concatenate([acc[hq * QROWS:(hq + 1) * QROWS, :], pad], axis=0)
            o_ref[:, hq * HEAD_DIM:(hq + 1) * HEAD_DIM] = full.astype(o_ref.dtype)


def attn_sample(q_all, kv_all, cache_k, cache_v, page_table, b_logit, o_prev):
    n_seq, n_pages = page_table.shape
    n_steps = n_pages // PAGES_PER_STEP
    rb0 = MP // S_PAD
    kvw = KV_HEADS * HEAD_DIM
    n_phys = cache_k.shape[0]
    ck = cache_k.reshape(n_phys, PAGE_SIZE * KV_HEADS, HEAD_DIM)
    cv = cache_v.reshape(n_phys, PAGE_SIZE * KV_HEADS, HEAD_DIM)
    bias = jnp.broadcast_to(b_logit.astype(F32)[:, None, None], (B_HEADS, QROWS, 1)).reshape(B_HEADS * QROWS, 1)

    def page(u):
        return lambda s, p, pt: (pt[s * n_pages + n_pages - 1 - (p * PAGES_PER_STEP + u)], 0, 0)

    page_specs = [pl.BlockSpec((1, PAGE_SIZE * KV_HEADS, HEAD_DIM), page(u)) for u in range(PAGES_PER_STEP)]
    grid_spec = pltpu.PrefetchScalarGridSpec(
        num_scalar_prefetch=1, grid=(n_seq, n_steps),
        in_specs=[pl.BlockSpec((S_PAD, B_HEADS * HEAD_DIM), lambda s, p, pt: (rb0 + s, 0)),
                  pl.BlockSpec((S_PAD, kvw), lambda s, p, pt: (rb0 + s, 0)),
                  pl.BlockSpec((S_PAD, kvw), lambda s, p, pt: (rb0 + s, 1))] + page_specs + page_specs + [
                  pl.BlockSpec((B_HEADS * QROWS, 1), lambda s, p, pt: (0, 0)),
                  pl.BlockSpec(memory_space=pl.ANY)],
        out_specs=pl.BlockSpec((S_PAD, B_HEADS * HEAD_DIM), lambda s, p, pt: (rb0 + s, 0)),
        scratch_shapes=[pltpu.VMEM((B_HEADS * QROWS, HEAD_DIM), F32),
                        pltpu.VMEM((B_HEADS * QROWS, 1), F32),
                        pltpu.VMEM((B_HEADS * QROWS, HEAD_DIM), F32)])
    args = [page_table.reshape(-1), q_all, kv_all, kv_all] + [ck] * PAGES_PER_STEP + [cv] * PAGES_PER_STEP + [bias, o_prev]
    return pl.pallas_call(
        functools.partial(_attn_sample_body, n_steps=n_steps),
        grid_spec=grid_spec,
        out_shape=jax.ShapeDtypeStruct(o_prev.shape, o_prev.dtype),
        input_output_aliases={len(args) - 1: 0},
        compiler_params=_cparams("arbitrary", "arbitrary"), name="attn_sample")(*args)


PACK_CHUNKS = D_MODEL // 2 // LANES
ROW_CHUNKS = D_MODEL // LANES
HI_MASK = 0xFFFF0000


def _router_body(*refs, n_src):
    g_ref, w_ref, xp_ref, idx_ref, gate_ref = refs[n_src:]
    x = _stream_tile(refs[:n_src])
    tr = x.shape[0]
    xn = x * lax.rsqrt(jnp.mean(x * x, axis=-1, keepdims=True) + RMS_EPS) * g_ref[...]
    xb = xn.astype(BF16)
    bits = pltpu.bitcast(xb.astype(F32), jnp.uint32)
    packed = bits[:, :D_MODEL // 2] | (bits[:, D_MODEL // 2:] >> 16)
    for c in range(PACK_CHUNKS):
        xp_ref[pl.ds(c, tr, stride=PACK_CHUNKS), :] = packed[:, c * LANES:(c + 1) * LANES]
    logits = _dot(xb, w_ref[...].astype(BF16))
    lane = lax.broadcasted_iota(jnp.int32, logits.shape, 1)
    lane_f = lane.astype(F32)
    ninf = -jnp.inf

    def first_max(vals):
        mx = jnp.max(vals, axis=1, keepdims=True)
        return mx, jnp.min(jnp.where(vals == mx, lane_f, float(LANES)), axis=1, keepdims=True)

    gl = jnp.where(lane < N_GROUPS, logits, ninf)
    gmax, grp = first_max(gl)
    g_gate = 1.0 / jnp.sum(jnp.exp(gl - gmax), axis=1, keepdims=True)
    e_lane = lane - N_GROUPS
    lane_grp = jnp.right_shift(e_lane, 3).astype(F32)
    in_grp = (e_lane >= 0) & (e_lane < N_EXPERTS) & (lane_grp == grp)
    el = jnp.where(in_grp, logits, ninf)
    t1, i1 = first_max(el)
    t2, i2 = first_max(jnp.where(lane_f == i1, ninf, el))
    e21 = jnp.exp(t2 - t1)
    g1 = g_gate / (1.0 + e21)
    g2 = g_gate * e21 / (1.0 + e21)
    idx_ref[...] = jnp.where(lane == 0, i1, jnp.where(lane == 1, i2, float(N_GROUPS))).astype(jnp.int32) - N_GROUPS
    gate_ref[...] = jnp.where(lane == 0, g1, jnp.where(lane == 1, g2, 0.0))


def router(src, g_norm, w_group, w_router, tr):
    m, d = M_ALL, D_MODEL
    w = jnp.pad(jnp.concatenate([w_group, w_router], axis=1).astype(F32), ((0, 0), (0, LANES - N_GROUPS - N_EXPERTS)))
    small = pl.BlockSpec((tr, LANES), lambda i: (i, 0))
    xp, idx, gate = pl.pallas_call(
        functools.partial(_router_body, n_src=len(src)), grid=(m // tr,),
        in_specs=_stream_specs(src, tr) + [pl.BlockSpec((1, d), lambda i: (0, 0)),
                                           pl.BlockSpec((d, LANES), lambda i: (0, 0))],
        out_specs=[pl.BlockSpec((tr * PACK_CHUNKS, LANES), lambda i: (i, 0)), small, small],
        out_shape=[jax.ShapeDtypeStruct((m * PACK_CHUNKS, LANES), jnp.uint32),
                   jax.ShapeDtypeStruct((m, LANES), jnp.int32), jax.ShapeDtypeStruct((m, LANES), F32)],
        compiler_params=_cparams("arbitrary"), name="router")(*src, g_norm.reshape(1, d), w)
    return xp, idx[:, :2], gate


BM = 1024
SUB = 256


def _moe_plan(idx):
    m = idx.shape[0]
    flat_e = idx.reshape(-1)
    onehot = (flat_e[:, None] == jnp.arange(N_EXPERTS, dtype=jnp.int32)[None, :]).astype(jnp.int32)
    csum = jnp.cumsum(onehot, axis=0)
    rank = jnp.take_along_axis(csum, flat_e[:, None], axis=1)[:, 0] - 1
    counts = csum[-1]
    nblk = (counts + BM - 1) // BM
    ends = jnp.cumsum(nblk)
    bstart = ends - nblk
    dest = (bstart[flat_e] * BM + rank).astype(jnp.int32)
    n_blocks = N_EXPERTS + (2 * m) // BM
    n_used = ends[-1]
    used = jnp.arange(n_blocks) < n_used
    blk = jnp.minimum(jnp.arange(n_blocks, dtype=jnp.int32), n_used - 1)
    be = jnp.minimum(jnp.searchsorted(ends, blk, side="right"), N_EXPERTS - 1).astype(jnp.int32)
    cnt = jnp.clip(counts[be] - (blk - bstart[be]) * BM, 0, BM)
    cnt = jnp.where(used, cnt, 0).astype(jnp.int32)
    n_rows = n_blocks * BM
    tok = jnp.arange(2 * m, dtype=jnp.int32) // 2
    row_src = jnp.zeros((n_rows,), jnp.int32).at[dest].set(tok)
    return dict(be=be, bx=blk.astype(jnp.int32), cnt=cnt, row_src=row_src, dest=dest, n_blocks=n_blocks, n_rows=n_rows)


def _gather_rows(idx_of, first, n, src_ref, dst_of, sem, rs):
    def start(r, carry):
        src_row = pl.multiple_of(idx_of(r) * rs, rs)
        pltpu.make_async_copy(src_ref.at[pl.ds(src_row, rs)], dst_of(r), sem).start()
        return carry

    lax.fori_loop(first, first + n, start, 0, unroll=8)


def _ffn_in_body(be_ref, bx_ref, cnt_ref, src_ref, xp_ref, wg_ref, wu_ref, o_ref, xg_ref, xs_ref, sem):
    del be_ref, bx_ref
    b = pl.program_id(0)
    cnt = cnt_ref[b]
    n_sub = (cnt + SUB - 1) // SUB
    sub_words = SUB * PACK_CHUNKS

    @pl.when((pl.program_id(1) == 0) & (cnt > 0))
    def _():
        for sub in range(BM // SUB):
            @pl.when(sub < n_sub)
            def _():
                _gather_rows(lambda r: src_ref[0, 0, r], sub * SUB, SUB, xp_ref,
                             lambda r: xg_ref.at[pl.ds(pl.multiple_of(r * PACK_CHUNKS, PACK_CHUNKS), PACK_CHUNKS)],
                             sem, PACK_CHUNKS)
        for sub in range(BM // SUB):
            @pl.when(sub < n_sub)
            def _():
                rows = pl.ds(sub * sub_words, sub_words)
                pltpu.make_async_copy(xp_ref.at[rows], xg_ref.at[rows], sem).wait()
        half = D_MODEL // 2
        for sub in range(BM // SUB):
            @pl.when(sub < n_sub)
            def _():
                rows = slice(sub * SUB, (sub + 1) * SUB)
                for c in range(PACK_CHUNKS):
                    w = xg_ref[pl.ds(sub * sub_words + c, SUB, stride=PACK_CHUNKS), :]
                    xs_ref[rows, c * LANES:(c + 1) * LANES] = pltpu.bitcast(w & jnp.uint32(HI_MASK), F32).astype(BF16)
                    xs_ref[rows, half + c * LANES:half + (c + 1) * LANES] = pltpu.bitcast(w << 16, F32).astype(BF16)

    @pl.when(cnt > 0)
    def _():
        wg = wg_ref[0, 0].astype(BF16)
        wu = wu_ref[0, 0].astype(BF16)
        for sub in range(BM // SUB):
            @pl.when(sub * SUB < cnt)
            def _():
                x = xs_ref[sub * SUB:(sub + 1) * SUB, :]
                g = _dot(x, wg)
                u = _dot(x, wu)
                o_ref[sub * SUB:(sub + 1) * SUB, :] = (g * (1.0 / (1.0 + jnp.exp(-g))) * u).astype(o_ref.dtype)


def _ffn_out_body(be_ref, bx_ref, cnt_ref, h_ref, w_ref, o_ref, wb_ref):
    del bx_ref
    b = pl.program_id(1)
    cnt = cnt_ref[b]
    new_w = (b == 0) | (be_ref[b] != be_ref[jnp.maximum(b - 1, 0)])

    @pl.when(new_w)
    def _():
        wb_ref[...] = w_ref[0, 0].astype(BF16)

    n_chunks = o_ref.shape[1]
    o2 = o_ref.reshape(BM * n_chunks, LANES)
    for sub in range(BM // SUB):
        @pl.when(sub * SUB < cnt)
        def _():
            y = _dot(h_ref[sub * SUB:(sub + 1) * SUB, :], wb_ref[...])
            for c in range(n_chunks):
                o2[pl.ds(sub * SUB * n_chunks + c, SUB, stride=n_chunks), :] = y[:, c * LANES:(c + 1) * LANES]


def expert_ffn(xp, plan, w_in, w_out, layer, tn_in=256, tn_out=1024):
    n_rows = plan["n_rows"]
    nb = plan["n_blocks"]
    d = D_MODEL
    nci = D_EXPERT // tn_in

    def col(c, b, cnt):
        return jnp.where(cnt[b] > 0, c, nci - 1)

    h_mid = pl.pallas_call(
        _ffn_in_body,
        grid_spec=pltpu.PrefetchScalarGridSpec(
            num_scalar_prefetch=3, grid=(nb, nci),
            in_specs=[pl.BlockSpec((1, 1, BM), lambda b, c, be, bx, cnt: (bx[b], 0, 0), memory_space=pltpu.SMEM),
                      pl.BlockSpec(memory_space=pl.ANY),
                      pl.BlockSpec((1, 1, d, tn_in), lambda b, c, be, bx, cnt: (layer, be[b], 0, col(c, b, cnt))),
                      pl.BlockSpec((1, 1, d, tn_in), lambda b, c, be, bx, cnt: (layer, be[b], 0, nci + col(c, b, cnt)))],
            out_specs=pl.BlockSpec((BM, tn_in), lambda b, c, be, bx, cnt: (bx[b], col(c, b, cnt))),
            scratch_shapes=[pltpu.VMEM((BM * PACK_CHUNKS, LANES), jnp.uint32), pltpu.VMEM((BM, d), BF16),
                            pltpu.SemaphoreType.DMA(())]),
        out_shape=jax.ShapeDtypeStruct((n_rows, D_EXPERT), BF16),
        compiler_params=_cparams("arbitrary", "arbitrary"), name="ffn_in")(
            plan["be"], plan["bx"], plan["cnt"], plan["row_src"].reshape(nb, 1, BM), xp, w_in, w_in)
    nco = d // tn_out
    oc = tn_out // LANES
    return pl.pallas_call(
        _ffn_out_body,
        grid_spec=pltpu.PrefetchScalarGridSpec(
            num_scalar_prefetch=3, grid=(nco, nb),
            in_specs=[pl.BlockSpec((BM, D_EXPERT), lambda c, b, be, bx, cnt: (bx[b], 0)),
                      pl.BlockSpec((1, 1, D_EXPERT, tn_out), lambda c, b, be, bx, cnt: (layer, be[b], 0, c))],
            out_specs=pl.BlockSpec((BM, oc, LANES), lambda c, b, be, bx, cnt: (bx[b], c, 0)),
            scratch_shapes=[pltpu.VMEM((D_EXPERT, tn_out), BF16)]),
        out_shape=jax.ShapeDtypeStruct((n_rows, ROW_CHUNKS, LANES), F32),
        compiler_params=_cparams("arbitrary", "arbitrary"), name="ffn_out")(
            plan["be"], plan["bx"], plan["cnt"], h_mid, w_out)


def _stream_cols(refs, cols):
    if len(refs) == 1:
        return refs[0][:, cols]
    tr = refs[0].shape[0]
    base = jnp.where(pl.program_id(0) < MP // tr, refs[0][:, cols], refs[1][:, cols])
    return base + refs[2][:, cols] if len(refs) == 3 else base


def _combine_body(dcur_ref, dnext_ref, *refs, n_src, n_norm, split_out, n_tiles):
    src_refs = refs[:n_src]
    ys_ref, gate_ref = refs[n_src:n_src + 2]
    rest = refs[n_src + 2:]
    g_ref = None
    if n_norm:
        g_ref, *rest = rest
    n_out = 2 if split_out else 1
    out_refs = rest[:n_out]
    norm_refs = rest[n_out:n_out + n_norm]
    y_ref, sem = rest[n_out + n_norm:]
    i = pl.program_id(0)
    tr = gate_ref.shape[0]
    tile_rows = tr * ROW_CHUNKS
    buf = jnp.bitwise_and(i, 1)

    def fetch(d_ref, s):
        _gather_rows(lambda e: d_ref[0, 0, e], 0, 2 * tr, ys_ref,
                     lambda e: y_ref.at[s, e & 1, pl.ds(pl.multiple_of((e >> 1) * ROW_CHUNKS, ROW_CHUNKS), ROW_CHUNKS)],
                     sem.at[s], ROW_CHUNKS)

    @pl.when(i == 0)
    def _():
        fetch(dcur_ref, 0)

    @pl.when(i + 1 < n_tiles)
    def _():
        fetch(dnext_ref, 1 - buf)

    for slot in range(2):
        pltpu.make_async_copy(ys_ref.at[pl.ds(0, tile_rows)], y_ref.at[buf, slot], sem.at[buf]).wait()
    g0 = gate_ref[:, 0:1]
    g1 = gate_ref[:, 1:2]

    def emit(o_ref):
        ss = jnp.zeros((tr, 1), F32)
        for c in range(ROW_CHUNKS):
            cols = slice(c * LANES, (c + 1) * LANES)
            moe = (g0 * y_ref[buf, 0, pl.ds(c, tr, stride=ROW_CHUNKS), :]
                   + g1 * y_ref[buf, 1, pl.ds(c, tr, stride=ROW_CHUNKS), :])
            hn = _stream_cols(src_refs, cols) + moe
            o_ref[:, cols] = hn
            ss = ss + jnp.sum(hn * hn, axis=1, keepdims=True)
        if n_norm:
            r = lax.rsqrt(ss * (1.0 / D_MODEL) + RMS_EPS)
            for c in range(ROW_CHUNKS):
                cols = slice(c * LANES, (c + 1) * LANES)
                xr = o_ref[:, cols] * r
                for j, n_ref in enumerate(norm_refs):
                    n_ref[:, cols] = (xr * g_ref[j:j + 1, cols]).astype(n_ref.dtype)

    if split_out:
        pl.when(i < n_tiles - 1)(lambda: emit(out_refs[0]))
        pl.when(i == n_tiles - 1)(lambda: emit(out_refs[1]))
    else:
        emit(out_refs[0])


def combine(src, y_sorted, dest, gate, gains, tr, split_out):
    m, d = M_ALL, D_MODEL
    n_tiles = m // tr
    n_norm = 0 if gains is None else gains.shape[0]
    row = pl.BlockSpec((tr, d), lambda i: (i, 0))
    dest3 = dest.reshape(n_tiles, 1, 2 * tr)
    in_specs = [pl.BlockSpec((1, 1, 2 * tr), lambda i: (i, 0, 0), memory_space=pltpu.SMEM),
                pl.BlockSpec((1, 1, 2 * tr), lambda i: (jnp.minimum(i + 1, n_tiles - 1), 0, 0), memory_space=pltpu.SMEM)]
    in_specs += _stream_specs(src, tr) + [pl.BlockSpec(memory_space=pl.ANY), pl.BlockSpec((tr, LANES), lambda i: (i, 0))]
    args = [dest3, dest3, *src, y_sorted, gate]
    if n_norm:
        in_specs.append(pl.BlockSpec((n_norm, d), lambda i: (0, 0)))
        args.append(gains)
    if split_out:
        assert n_norm == 0 and tr == MS
        out_specs = [pl.BlockSpec((tr, d), lambda i: (jnp.minimum(i, n_tiles - 2), 0)), pl.BlockSpec((tr, d), lambda i: (0, 0))]
        out_shape = [jax.ShapeDtypeStruct((MP, d), F32), jax.ShapeDtypeStruct((MS, d), F32)]
    else:
        out_specs = [row] * (1 + n_norm)
        out_shape = [jax.ShapeDtypeStruct((m, d), F32)] + [jax.ShapeDtypeStruct((m, d), BF16)] * n_norm
    return pl.pallas_call(
        functools.partial(_combine_body, n_src=len(src), n_norm=n_norm, split_out=split_out, n_tiles=n_tiles),
        grid=(n_tiles,), in_specs=in_specs, out_specs=out_specs, out_shape=out_shape,
        scratch_shapes=[pltpu.VMEM((2, 2, tr * ROW_CHUNKS, LANES), F32), pltpu.SemaphoreType.DMA((2,))],
        compiler_params=_cparams("arbitrary"), name="combine")(*args)


def hier_moe_layer(src, g_norm, w_group, w_router, w_in, w_out, layer, next_gains, tr, split_out=False):
    xp, idx, gate = router(src, g_norm, w_group, w_router, tr)
    plan = _moe_plan(idx)
    y_sorted = expert_ffn(xp, plan, w_in, w_out, layer)
    return combine(src, y_sorted.reshape(plan["n_rows"] * ROW_CHUNKS, LANES), plan["dest"], gate, next_gains, tr,
                   split_out)


TM = 640
TR = 128


def kernel(x_prompt, x_sample, state_C, state_n, state_m, cache_k, cache_v, page_table, norm_mix, w_in_a, b_gate_a,
           g_hnorm_a, w_out_a, g_kv, w_kv, g_knorm, w_q_b, g_qnorm_b, b_logit_b, w_out_b, norm_ffn, w_group,
           w_router, w_moe_in, w_moe_out):
    xs = jnp.pad(x_sample, ((0, 0), (0, S_PAD - DEC_SEQ), (0, 0))).reshape(MS, D_MODEL)
    x_in = (x_prompt.reshape(MP, D_MODEL), xs)

    (xn0,) = rms_norm(x_in, norm_mix[0:1], [BF16], TR)
    p_main = matmul(xn0, w_in_a[0], A_MAIN, F32, TM, 512)
    gates = mlstm_gates(xn0, w_in_a[0][:, A_MAIN:], b_gate_a[0], TM)
    hg, pc, pn, pm = mlstm(p_main, gates, g_hnorm_a[0], None, 0, BATCH, SEQ, SEQ, 256, 256, MLSTM_SUB)
    hg, sc, sn, sm = mlstm(p_main, gates, g_hnorm_a[0], hg, MP, DEC_BATCH, S_PAD, DEC_SEQ, S_PAD, LANES, LANES,
                           init=(state_C[0], state_n[0], state_m[0]))
    y_mix = matmul(hg, w_out_a[0], D_MODEL, F32, TM, 512)
    h1, xn1, xkv = hier_moe_layer(x_in + (y_mix,), norm_ffn[0], w_group[0], w_router[0], w_moe_in, w_moe_out, 0,
                                  jnp.stack([norm_mix[1], g_kv]), TR)

    kv_all = matmul(xkv, w_kv, 2 * KV_HEADS * HEAD_DIM, F32, TM, 512, mode="headnorm", gain=g_knorm,
                    n_norm_tiles=KV_HEADS * HEAD_DIM // 512)
    q_all = matmul(xn1, w_q_b[0], B_HEADS * HEAD_DIM, BF16, TM, 512, mode="headnorm", gain=g_qnorm_b[0],
                   n_norm_tiles=B_HEADS * HEAD_DIM // 512)
    o_all = attn_prompt(q_all, kv_all, b_logit_b[0])
    o_all = attn_sample(q_all, kv_all, cache_k, cache_v, page_table, b_logit_b[0], o_all)
    h2 = matmul(o_all, w_out_b[0], D_MODEL, F32, TM, 512, mode="resid", resid=h1)
    y_p, y_s = hier_moe_layer((h2,), norm_ffn[1], w_group[1], w_router[1], w_moe_in, w_moe_out, 1, None, TR,
                              split_out=True)

    def sample_rows(a):
        return a[MP:].reshape((DEC_BATCH, S_PAD) + a.shape[1:])[:, :DEC_SEQ]

    kvw = KV_HEADS * HEAD_DIM
    y_prompt = y_p.reshape(BATCH, SEQ, D_MODEL)
    y_sample = y_s.reshape(DEC_BATCH, S_PAD, D_MODEL)[:, :DEC_SEQ]
    prompt_k = kv_all[:MP, :kvw].reshape(BATCH, SEQ, KV_HEADS, HEAD_DIM)
    prompt_v = kv_all[:MP, kvw:].reshape(BATCH, SEQ, KV_HEADS, HEAD_DIM)
    sample_k = sample_rows(kv_all[:, :kvw]).reshape(DEC_BATCH, DEC_SEQ, KV_HEADS, HEAD_DIM)
    sample_v = sample_rows(kv_all[:, kvw:]).reshape(DEC_BATCH, DEC_SEQ, KV_HEADS, HEAD_DIM)
    return (y_prompt, y_sample, pc[None], pn[None], pm[None], prompt_k, prompt_v,
            sc[None], sn[None], sm[None], sample_k, sample_v)
```

```python
import functools

import jax
import jax.numpy as jnp
from jax import lax
from jax.experimental import pallas as pl
from jax.experimental.pallas import tpu as pltpu

F32 = jnp.float32
BF16 = jnp.bfloat16

D_MODEL = 4096
BATCH = 4
SEQ = 2048
DEC_BATCH = 8
DEC_SEQ = 4
PAGE_SIZE = 128
A_HEADS = 8
A_QK_DIM = 256
A_V_DIM = 512
A_QK = A_HEADS * A_QK_DIM
A_VD = A_HEADS * A_V_DIM
A_MAIN = 2 * A_QK + A_VD + D_MODEL
HEAD_DIM = 128
B_HEADS = 32
KV_HEADS = 8
Q_PER_KV = B_HEADS // KV_HEADS
N_GROUPS = 4
EXPERTS_PER_GROUP = 8
N_EXPERTS = N_GROUPS * EXPERTS_PER_GROUP
D_EXPERT = 1024
RMS_EPS = 1e-6

LANES = 128
MP = BATCH * SEQ
S_PAD = 16
MS = DEC_BATCH * S_PAD
M_ALL = MP + MS
NEG_BIG = -1e30
VMEM_LIMIT = 56 * 1024 * 1024

HIGHEST = lax.Precision.HIGHEST


def _cparams(*sem):
    return pltpu.CompilerParams(dimension_semantics=sem, vmem_limit_bytes=VMEM_LIMIT)


def _dot(a, b, precision=None):
    return jnp.dot(a, b, preferred_element_type=F32, precision=precision)


def _dot_nt(a, b):
    return lax.dot_general(a, b, (((1,), (1,)), ((), ())), preferred_element_type=F32)


def _log_sigmoid(x):
    return jnp.minimum(x, 0.0) - jnp.log1p(jnp.exp(-jnp.abs(x)))


def _stream_specs(src, tr):
    row = pl.BlockSpec((tr, D_MODEL), lambda i: (i, 0))
    if len(src) == 1:
        return [row]
    assert tr == MS and MP % tr == 0
    last_prompt = MP // tr - 1
    return [pl.BlockSpec((tr, D_MODEL), lambda i: (jnp.minimum(i, last_prompt), 0)),
            pl.BlockSpec((tr, D_MODEL), lambda i: (0, 0))] + [row] * (len(src) - 2)


def _stream_tile(refs):
    if len(refs) == 1:
        return refs[0][...]
    tr = refs[0].shape[0]
    base = jnp.where(pl.program_id(0) < MP // tr, refs[0][...], refs[1][...])
    return base + refs[2][...] if len(refs) == 3 else base


def _rms_body(*refs, n_src):
    g_ref, *out_refs = refs[n_src:]
    x = _stream_tile(refs[:n_src])
    xr = x * lax.rsqrt(jnp.mean(x * x, axis=-1, keepdims=True) + RMS_EPS)
    for j, o_ref in enumerate(out_refs):
        o_ref[...] = (xr * g_ref[j:j + 1, :]).astype(o_ref.dtype)


def rms_norm(src, gains, out_dtypes, tr):
    n = len(out_dtypes)
    row = pl.BlockSpec((tr, D_MODEL), lambda i: (i, 0))
    return pl.pallas_call(
        functools.partial(_rms_body, n_src=len(src)), grid=(M_ALL // tr,),
        in_specs=_stream_specs(src, tr) + [pl.BlockSpec((n, D_MODEL), lambda i: (0, 0))],
        out_specs=[row] * n,
        out_shape=[jax.ShapeDtypeStruct((M_ALL, D_MODEL), dt) for dt in out_dtypes],
        compiler_params=_cparams("arbitrary"), name="rms_norm")(*src, gains)


def _mm_body(x_ref, w_ref, *rest, mode, n_norm_tiles, w_transposed):
    if mode == "resid":
        r_ref, o_ref, wb_ref = rest
    elif mode == "headnorm":
        g_ref, o_ref, wb_ref = rest
    else:
        o_ref, wb_ref = rest
    j = pl.program_id(0)

    @pl.when(pl.program_id(1) == 0)
    def _():
        wb_ref[...] = w_ref[...].astype(BF16)

    acc = _dot_nt(x_ref[...], wb_ref[...]) if w_transposed else _dot(x_ref[...], wb_ref[...])
    if mode == "plain":
        o_ref[...] = acc.astype(o_ref.dtype)
    elif mode == "resid":
        o_ref[...] = r_ref[...] + acc
    else:
        @pl.when(j < n_norm_tiles)
        def _():
            for c in range(acc.shape[1] // HEAD_DIM):
                a = acc[:, c * HEAD_DIM:(c + 1) * HEAD_DIM]
                y = a * lax.rsqrt(jnp.mean(a * a, axis=-1, keepdims=True) + RMS_EPS) * g_ref[...]
                o_ref[:, c * HEAD_DIM:(c + 1) * HEAD_DIM] = y.astype(o_ref.dtype)

        @pl.when(j >= n_norm_tiles)
        def _():
            o_ref[...] = acc.astype(o_ref.dtype)


def matmul(x, w, n_out, out_dtype, tm, tn, mode="plain", resid=None, gain=None, n_norm_tiles=0, w_transposed=False):
    m, k = x.shape
    w_spec = pl.BlockSpec((tn, k), lambda j, i: (j, 0)) if w_transposed else pl.BlockSpec((k, tn), lambda j, i: (0, j))
    in_specs = [pl.BlockSpec((tm, k), lambda j, i: (i, 0)), w_spec]
    args = [x, w]
    if mode == "resid":
        in_specs.append(pl.BlockSpec((tm, tn), lambda j, i: (i, j)))
        args.append(resid)
    elif mode == "headnorm":
        in_specs.append(pl.BlockSpec((1, HEAD_DIM), lambda j, i: (0, 0)))
        args.append(gain.reshape(1, HEAD_DIM))
    return pl.pallas_call(
        functools.partial(_mm_body, mode=mode, n_norm_tiles=n_norm_tiles, w_transposed=w_transposed),
        grid=(n_out // tn, m // tm),
        in_specs=in_specs,
        out_specs=pl.BlockSpec((tm, tn), lambda j, i: (i, j)),
        out_shape=jax.ShapeDtypeStruct((m, n_out), out_dtype),
        scratch_shapes=[pltpu.VMEM((tn, k) if w_transposed else (k, tn), BF16)],
        compiler_params=_cparams("arbitrary", "arbitrary"), name="matmul_" + mode)(*args)


def _gates_body(x_ref, w_ref, b_ref, o_ref):
    n_gate, k = w_ref.shape
    w = jnp.concatenate([w_ref[...], jnp.zeros((LANES - n_gate, k), F32)], axis=0).astype(BF16)
    pre = _dot_nt(x_ref[...], w) + b_ref[...]
    lane = lax.broadcasted_iota(jnp.int32, pre.shape, 1)
    o_ref[...] = jnp.where(lane < A_HEADS, pre, jnp.where(lane < 2 * A_HEADS, _log_sigmoid(pre), 0.0))


def mlstm_gates(x, w_in_t, b_gate, tm):
    m, k = x.shape
    n_gate = 2 * A_HEADS
    assert A_MAIN % n_gate == 0 and w_in_t.shape[0] - A_MAIN == n_gate
    b = jnp.pad(b_gate.astype(F32), (0, LANES - b_gate.shape[0])).reshape(1, LANES)
    return pl.pallas_call(
        _gates_body, grid=(m // tm,),
        in_specs=[pl.BlockSpec((tm, k), lambda i: (i, 0)), pl.BlockSpec((n_gate, k), lambda i: (A_MAIN // n_gate, 0)),
                  pl.BlockSpec((1, LANES), lambda i: (0, 0))],
        out_specs=pl.BlockSpec((tm, LANES), lambda i: (i, 0)),
        out_shape=jax.ShapeDtypeStruct((m, LANES), F32),
        compiler_params=_cparams("arbitrary"), name="mlstm_gates")(x, w_in_t, b)


MLSTM_SUB = 64
def _mlstm_body(*refs, chunk, lb, sub, t_valid, t_total, has_init):
    if has_init:
        q_ref, k_ref, v_ref, o_ref, ga_ref, gh_ref, c0_ref, n0_ref, m0_ref, hg_ref, c_ref, n_ref, m_ref = refs
    else:
        q_ref, k_ref, v_ref, o_ref, ga_ref, gh_ref, hg_ref, c_ref, n_ref, m_ref = refs
    h = pl.program_id(1)
    c = pl.program_id(2)

    @pl.when(c == 0)
    def _():
        if has_init:
            c_ref[...] = c0_ref[...]
            n_ref[...] = n0_ref[...]
            m_ref[...] = m0_ref[...]
        else:
            c_ref[...] = jnp.zeros_like(c_ref)
            n_ref[...] = jnp.zeros_like(n_ref)
            m_ref[...] = jnp.zeros_like(m_ref)

    def load(ref):
        x = ref[...]
        if lb == chunk:
            return x
        return jnp.concatenate([x, jnp.zeros((chunk - lb, x.shape[1]), x.dtype)], axis=0)

    q = load(q_ref).astype(BF16)
    k = load(k_ref).astype(BF16)
    v = load(v_ref).astype(BF16)
    ga = load(ga_ref)
    c_state = c_ref[0, 0]
    n_state = n_ref[0, 0]
    m_state = m_ref[0, 0][:, 0:1]
    n_sub = chunk // sub
    sub_shift = sub.bit_length() - 1

    lane = lax.broadcasted_iota(jnp.int32, (chunk, LANES), 1)
    i_col = jnp.sum(jnp.where(lane == h, ga, 0.0), axis=1, keepdims=True)
    f_col = jnp.sum(jnp.where(lane == h + A_HEADS, ga, 0.0), axis=1, keepdims=True)
    if t_valid < t_total:
        t_idx = lax.broadcasted_iota(jnp.int32, (chunk, 1), 0) + c * chunk
        i_col = jnp.where(t_idx < t_valid, i_col, NEG_BIG)
        f_col = jnp.where(t_idx < t_valid, f_col, 0.0)

    row_i = lax.broadcasted_iota(jnp.int32, (chunk, chunk), 0)
    col_i = lax.broadcasted_iota(jnp.int32, (chunk, chunk), 1)
    causal = (col_i <= row_i) & (jnp.right_shift(row_i, sub_shift) == jnp.right_shift(col_i, sub_shift))
    g2 = jnp.where(lane == 0, i_col, jnp.where(lane == 1, f_col, 0.0))
    cum = _dot(causal.astype(F32), g2, precision=HIGHEST)
    g3 = jnp.where(lane == 1, cum, g2)
    g3t = g3.T
    i_row = g3t[0:1, :]
    b_row = g3t[1:2, :]
    b_col = g3[:, 1:2]
    sub_of_row = jnp.right_shift(lax.broadcasted_iota(jnp.int32, (1, chunk), 1), sub_shift)
    sub_of_col = jnp.right_shift(lax.broadcasted_iota(jnp.int32, (chunk, 1), 0), sub_shift)

    m_at = [m_state]
    b_last = []
    for j in range(n_sub):
        bl = b_col[(j + 1) * sub - 1:(j + 1) * sub, :]
        g = bl - b_col[j * sub:(j + 1) * sub, :] + i_col[j * sub:(j + 1) * sub, :]
        b_last.append(bl)
        m_at.append(jnp.maximum(bl + m_at[j], jnp.max(g, axis=0, keepdims=True)))
    m_col = jnp.concatenate([jnp.broadcast_to(m_at[j], (sub, 1)) for j in range(n_sub)], axis=0)

    dmat = jnp.where(causal, b_col - b_row + i_row, NEG_BIG)
    inter = b_col + m_col
    m_t = jnp.maximum(inter, jnp.max(dmat, axis=1, keepdims=True))
    w_intra = jnp.exp(dmat - m_t)
    w_inter = jnp.exp(inter - m_t)
    scale = A_QK_DIM ** -0.5
    s = _dot_nt(q, k) * scale * w_intra
    intra = _dot(s.astype(BF16), v)
    s_sum = jnp.sum(s, axis=1, keepdims=True)

    kf = k.astype(F32)
    kt = kf.T
    qc_parts = []
    qn_parts = []
    c_cur = c_state
    n_cur = n_state
    for j in range(n_sub):
        qj = q[j * sub:(j + 1) * sub]
        qc_parts.append(_dot(qj, c_cur.astype(BF16)))
        qn_parts.append(jnp.sum(qj.astype(F32) * n_cur.astype(BF16).astype(F32), axis=1, keepdims=True))
        m_new = m_at[j + 1]
        wk_row = jnp.where(sub_of_row == j, jnp.exp(b_last[j] - b_row + i_row - m_new), 0.0)
        wk_col = jnp.where(sub_of_col == j, jnp.exp(b_last[j] - b_col + i_col - m_new), 0.0)
        decay = jnp.exp(b_last[j] + m_at[j] - m_new)
        c_cur = decay * c_cur + _dot((kt * wk_row).astype(BF16), v)
        n_cur = decay * n_cur + jnp.sum(kf * wk_col.astype(BF16).astype(F32), axis=0, keepdims=True)
    num = w_inter * (jnp.concatenate(qc_parts, axis=0) * scale) + intra
    den = w_inter * (jnp.concatenate(qn_parts, axis=0) * scale) + s_sum
    hh = num / jnp.maximum(jnp.abs(den), jnp.exp(-m_t))

    hn = hh * lax.rsqrt(jnp.mean(hh * hh, axis=1, keepdims=True) + RMS_EPS) * gh_ref[0]
    og = o_ref[...].astype(F32)
    out = hn[0:lb] * (1.0 / (1.0 + jnp.exp(-og)))
    hg_ref[...] = out.astype(hg_ref.dtype)

    c_ref[0, 0] = c_cur
    n_ref[0, 0] = n_cur
    m_ref[0, 0] = jnp.broadcast_to(m_at[n_sub], (1, LANES))


def mlstm(p_main, gates, g_hnorm, hg_prev, row0, n_seq, t_total, t_valid, lb, chunk, sub, init=None):
    m = p_main.shape[0]
    nc = t_total // lb
    rb0 = row0 // lb

    def rows(b, h, c):
        return rb0 + b * nc + c

    qk_blocks = A_QK // A_QK_DIM
    in_specs = [
        pl.BlockSpec((lb, A_QK_DIM), lambda b, h, c: (rows(b, h, c), h)),
        pl.BlockSpec((lb, A_QK_DIM), lambda b, h, c: (rows(b, h, c), qk_blocks + h)),
        pl.BlockSpec((lb, A_V_DIM), lambda b, h, c: (rows(b, h, c), 2 * A_QK // A_V_DIM + h)),
        pl.BlockSpec((lb, A_V_DIM), lambda b, h, c: (rows(b, h, c), (2 * A_QK + A_VD) // A_V_DIM + h)),
        pl.BlockSpec((lb, LANES), lambda b, h, c: (rows(b, h, c), 0)),
        pl.BlockSpec((1, 1, A_V_DIM), lambda b, h, c: (h, 0, 0)),
    ]
    args = [p_main, p_main, p_main, p_main, gates, g_hnorm.reshape(A_HEADS, 1, A_V_DIM)]
    state_specs = [
        pl.BlockSpec((1, 1, A_QK_DIM, A_V_DIM), lambda b, h, c: (b, h, 0, 0)),
        pl.BlockSpec((1, 1, 1, A_QK_DIM), lambda b, h, c: (b, h, 0, 0)),
        pl.BlockSpec((1, 1, 1, LANES), lambda b, h, c: (b, h, 0, 0)),
    ]
    has_init = init is not None
    if has_init:
        c0, n0, m0 = init
        in_specs += state_specs
        args += [c0.astype(F32), n0.astype(F32).reshape(n_seq, A_HEADS, 1, A_QK_DIM),
                 jnp.broadcast_to(m0.astype(F32)[:, :, None, None], (n_seq, A_HEADS, 1, LANES))]
    aliases = {}
    if hg_prev is not None:
        in_specs.append(pl.BlockSpec(memory_space=pl.ANY))
        args.append(hg_prev)
        aliases = {len(args) - 1: 0}

    def body(*refs):
        if hg_prev is not None:
            n_in = len(args)
            refs = refs[:n_in - 1] + refs[n_in:]
        _mlstm_body(*refs, chunk=chunk, lb=lb, sub=sub, t_valid=t_valid, t_total=t_total, has_init=has_init)

    hg, c_out, n_out, m_out = pl.pallas_call(
        body, grid=(n_seq, A_HEADS, nc),
        in_specs=in_specs,
        out_specs=[pl.BlockSpec((lb, A_V_DIM), lambda b, h, c: (rows(b, h, c), h))] + state_specs,
        out_shape=[jax.ShapeDtypeStruct((m, A_VD), BF16),
                   jax.ShapeDtypeStruct((n_seq, A_HEADS, A_QK_DIM, A_V_DIM), F32),
                   jax.ShapeDtypeStruct((n_seq, A_HEADS, 1, A_QK_DIM), F32),
                   jax.ShapeDtypeStruct((n_seq, A_HEADS, 1, LANES), F32)],
        input_output_aliases=aliases,
        compiler_params=_cparams("arbitrary", "arbitrary", "arbitrary"), name="mlstm")(*args)
    return hg, c_out, n_out.reshape(n_seq, A_HEADS, A_QK_DIM), m_out[:, :, 0, 0]


def _sb_block(z, tri, carry, mask):
    l1p = jnp.log(1.0 + jnp.exp(-jnp.abs(z)))
    ls = jnp.minimum(z, 0.0) - l1p
    u = ls - z
    if mask is not None:
        u = jnp.where(mask, u, 0.0)
    u_hi = u.astype(BF16)
    u_lo = (u - u_hi.astype(F32)).astype(BF16)
    within = _dot(u_hi, tri) + _dot(u_lo, tri)
    a = jnp.exp(ls + within + carry)
    if mask is not None:
        a = jnp.where(mask, a, 0.0)
    return a, carry + within[:, 0:1] + u[:, 0:1]


def _strict_tri(tk):
    r = lax.broadcasted_iota(jnp.int32, (tk, tk), 0)
    c = lax.broadcasted_iota(jnp.int32, (tk, tk), 1)
    return (r > c).astype(BF16)


def _attn_prompt_body(q_ref, k_ref, v_ref, b_ref, o_ref, *, tq, tk):
    qi = pl.program_id(2)
    scale = HEAD_DIM ** -0.5
    tri = _strict_tri(tk)
    t_loc = lax.broadcasted_iota(jnp.int32, (tq, tk), 0)
    s_loc = lax.broadcasted_iota(jnp.int32, (tq, tk), 1)
    nkb = tq // tk
    n_half = 2
    hpg = Q_PER_KV // n_half
    rows = hpg * tq

    def heads(g0, fn):
        return jnp.concatenate([fn(g) for g in range(g0, g0 + hpg)], axis=0)

    qs = [heads(s * hpg, lambda g: q_ref[:, g * HEAD_DIM:(g + 1) * HEAD_DIM]) for s in range(n_half)]
    bias = [heads(s * hpg, lambda g: jnp.broadcast_to(b_ref[0, :, g:g + 1], (tq, 1))) for s in range(n_half)]

    def kv_block(ref, j):
        return ref[pl.ds(pl.multiple_of(j * tk, tk), tk), :].astype(BF16)

    def logits(s, j):
        return _dot_nt(qs[s], kv_block(k_ref, j)) * scale + bias[s]

    def visit_diag(j, state, diag_off):
        mask = jnp.concatenate([(s_loc + diag_off * tk) < t_loc] * hpg, axis=0)
        vb = kv_block(v_ref, j)
        new = []
        for s in range(n_half):
            carry, acc = state[s]
            a, carry = _sb_block(logits(s, j), tri, carry, mask)
            new.append((carry, acc + _dot(a.astype(BF16), vb)))
        return new

    state = [(jnp.zeros((rows, 1), F32), jnp.zeros((rows, HEAD_DIM), F32)) for _ in range(n_half)]
    for d in reversed(range(nkb)):
        state = visit_diag(qi * nkb + d, state, d)

    n_before = qi * nkb

    def body(it, st):
        j = n_before - 1 - it
        z_next = [logits(s, jnp.maximum(j - 1, 0)) for s in range(n_half)]
        vb = kv_block(v_ref, st[0][4])
        new = []
        for s in range(n_half):
            carry, acc, z_cur, a_prev, _ = st[s]
            acc = acc + _dot(a_prev, vb)
            a, carry = _sb_block(z_cur, tri, carry, None)
            new.append((carry, acc, z_next[s], a.astype(BF16), j))
        return tuple(new)

    first = jnp.maximum(n_before - 1, 0)
    init = tuple((state[s][0], state[s][1], logits(s, first), jnp.zeros((rows, tk), BF16), first) for s in range(n_half))
    final = lax.fori_loop(0, n_before, body, init)
    vb = kv_block(v_ref, final[0][4])
    for s in range(n_half):
        acc = final[s][1] + _dot(final[s][3], vb)
        for g in range(hpg):
            col = (s * hpg + g) * HEAD_DIM
            o_ref[:, col:col + HEAD_DIM] = acc[g * tq:(g + 1) * tq].astype(o_ref.dtype)


def attn_prompt(q_all, kv_all, b_logit, tq=256, tk=128):
    m = q_all.shape[0]
    nq = SEQ // tq
    gw = Q_PER_KV * HEAD_DIM
    bias = b_logit.astype(F32).reshape(KV_HEADS, 1, Q_PER_KV)
    return pl.pallas_call(
        functools.partial(_attn_prompt_body, tq=tq, tk=tk),
        grid=(BATCH, KV_HEADS, nq),
        in_specs=[pl.BlockSpec((tq, gw), lambda b, h, i: (b * nq + i, h)),
                  pl.BlockSpec((SEQ, HEAD_DIM), lambda b, h, i: (b, h)),
                  pl.BlockSpec((SEQ, HEAD_DIM), lambda b, h, i: (b, KV_HEADS + h)),
                  pl.BlockSpec((1, 1, Q_PER_KV), lambda b, h, i: (h, 0, 0))],
        out_specs=pl.BlockSpec((tq, gw), lambda b, h, i: (b * nq + i, h)),
        out_shape=jax.ShapeDtypeStruct((m, B_HEADS * HEAD_DIM), BF16),
        compiler_params=_cparams("arbitrary", "arbitrary", "arbitrary"), name="attn_prompt")(q_all, kv_all, kv_all, bias)


QROWS = 8
PAGES_PER_STEP = 8


def _attn_sample_body(pt_ref, q_ref, kn_ref, vn_ref, *rest, n_steps):
    ck_refs = rest[:PAGES_PER_STEP]
    cv_refs = rest[PAGES_PER_STEP:2 * PAGES_PER_STEP]
    b_ref, oin_ref, o_ref, qs_ref, car_ref, acc_ref = rest[2 * PAGES_PER_STEP:]
    del pt_ref, oin_ref
    p = pl.program_id(1)
    scale = HEAD_DIM ** -0.5
    rows_h = Q_PER_KV * QROWS
    tri = _strict_tri(PAGE_SIZE)

    def process(get_k, get_v, mask, car, acc):
        qs = qs_ref[...]
        zs = [_dot_nt(qs[kh * rows_h:(kh + 1) * rows_h].astype(BF16), get_k(kh).astype(BF16)) for kh in range(KV_HEADS)]
        z = jnp.concatenate(zs, axis=0) * scale + b_ref[...]
        a, car = _sb_block(z, tri, car, mask)
        ab = a.astype(BF16)
        accs = [acc[kh * rows_h:(kh + 1) * rows_h] + _dot(ab[kh * rows_h:(kh + 1) * rows_h], get_v(kh).astype(BF16))
                for kh in range(KV_HEADS)]
        return car, jnp.concatenate(accs, axis=0)

    @pl.when(p == 0)
    def _():
        for hq in range(B_HEADS):
            qs_ref[hq * QROWS:(hq + 1) * QROWS, :] = q_ref[:, hq * HEAD_DIM:(hq + 1) * HEAD_DIM].astype(F32)[0:QROWS]
        pad = jnp.zeros((PAGE_SIZE - S_PAD, HEAD_DIM), F32)
        n_rows = B_HEADS * QROWS
        t_loc = jnp.bitwise_and(lax.broadcasted_iota(jnp.int32, (n_rows, PAGE_SIZE), 0), QROWS - 1)
        s_loc = lax.broadcasted_iota(jnp.int32, (n_rows, PAGE_SIZE), 1)
        car, acc = process(lambda kh: jnp.concatenate([kn_ref[:, kh * HEAD_DIM:(kh + 1) * HEAD_DIM], pad], axis=0),
                           lambda kh: jnp.concatenate([vn_ref[:, kh * HEAD_DIM:(kh + 1) * HEAD_DIM], pad], axis=0),
                           s_loc < t_loc, jnp.zeros((n_rows, 1), F32), jnp.zeros((n_rows, HEAD_DIM), F32))
        car_ref[...] = car
        acc_ref[...] = acc

    car = car_ref[...]
    acc = acc_ref[...]
    for ck_ref, cv_ref in zip(ck_refs, cv_refs):
        car, acc = process(lambda kh: ck_ref[0, pl.ds(kh, PAGE_SIZE, stride=KV_HEADS), :],
                           lambda kh: cv_ref[0, pl.ds(kh, PAGE_SIZE, stride=KV_HEADS), :], None, car, acc)
    car_ref[...] = car
    acc_ref[...] = acc

    @pl.when(p == n_steps - 1)
    def _():
        pad = jnp.zeros((S_PAD - QROWS, HEAD_DIM), F32)
        for hq in range(B_HEADS):
            full = jnp.concatenate([acc[hq * QROWS:(hq + 1) * QROWS, :], pad], axis=0)
            o_ref[:, hq * HEAD_DIM:(hq + 1) * HEAD_DIM] = full.astype(o_ref.dtype)


def attn_sample(q_all, kv_all, cache_k, cache_v, page_table, b_logit, o_prev):
    n_seq, n_pages = page_table.shape
    n_steps = n_pages // PAGES_PER_STEP
    rb0 = MP // S_PAD
    kvw = KV_HEADS * HEAD_DIM
    n_phys = cache_k.shape[0]
    ck = cache_k.reshape(n_phys, PAGE_SIZE * KV_HEADS, HEAD_DIM)
    cv = cache_v.reshape(n_phys, PAGE_SIZE * KV_HEADS, HEAD_DIM)
    bias = jnp.broadcast_to(b_logit.astype(F32)[:, None, None], (B_HEADS, QROWS, 1)).reshape(B_HEADS * QROWS, 1)

    def page(u):
        return lambda s, p, pt: (pt[s * n_pages + n_pages - 1 - (p * PAGES_PER_STEP + u)], 0, 0)

    page_specs = [pl.BlockSpec((1, PAGE_SIZE * KV_HEADS, HEAD_DIM), page(u)) for u in range(PAGES_PER_STEP)]
    grid_spec = pltpu.PrefetchScalarGridSpec(
        num_scalar_prefetch=1, grid=(n_seq, n_steps),
        in_specs=[pl.BlockSpec((S_PAD, B_HEADS * HEAD_DIM), lambda s, p, pt: (rb0 + s, 0)),
                  pl.BlockSpec((S_PAD, kvw), lambda s, p, pt: (rb0 + s, 0)),
                  pl.BlockSpec((S_PAD, kvw), lambda s, p, pt: (rb0 + s, 1))] + page_specs + page_specs + [
                  pl.BlockSpec((B_HEADS * QROWS, 1), lambda s, p, pt: (0, 0)),
                  pl.BlockSpec(memory_space=pl.ANY)],
        out_specs=pl.BlockSpec((S_PAD, B_HEADS * HEAD_DIM), lambda s, p, pt: (rb0 + s, 0)),
        scratch_shapes=[pltpu.VMEM((B_HEADS * QROWS, HEAD_DIM), F32),
                        pltpu.VMEM((B_HEADS * QROWS, 1), F32),
                        pltpu.VMEM((B_HEADS * QROWS, HEAD_DIM), F32)])
    args = [page_table.reshape(-1), q_all, kv_all, kv_all] + [ck] * PAGES_PER_STEP + [cv] * PAGES_PER_STEP + [bias, o_prev]
    return pl.pallas_call(
        functools.partial(_attn_sample_body, n_steps=n_steps),
        grid_spec=grid_spec,
        out_shape=jax.ShapeDtypeStruct(o_prev.shape, o_prev.dtype),
        input_output_aliases={len(args) - 1: 0},
        compiler_params=_cparams("arbitrary", "arbitrary"), name="attn_sample")(*args)


PACK_CHUNKS = D_MODEL // 2 // LANES
ROW_CHUNKS = D_MODEL // LANES
HI_MASK = 0xFFFF0000


def _router_body(*refs, n_src):
    g_ref, w_ref, xp_ref, idx_ref, gate_ref = refs[n_src:]
    x = _stream_tile(refs[:n_src])
    tr = x.shape[0]
    xn = x * lax.rsqrt(jnp.mean(x * x, axis=-1, keepdims=True) + RMS_EPS) * g_ref[...]
    xb = xn.astype(BF16)
    bits = pltpu.bitcast(xb.astype(F32), jnp.uint32)
    packed = bits[:, :D_MODEL // 2] | (bits[:, D_MODEL // 2:] >> 16)
    for c in range(PACK_CHUNKS):
        xp_ref[pl.ds(c, tr, stride=PACK_CHUNKS), :] = packed[:, c * LANES:(c + 1) * LANES]
    logits = _dot(xb, w_ref[...].astype(BF16))
    lane = lax.broadcasted_iota(jnp.int32, logits.shape, 1)
    lane_f = lane.astype(F32)
    ninf = -jnp.inf

    def first_max(vals):
        mx = jnp.max(vals, axis=1, keepdims=True)
        return mx, jnp.min(jnp.where(vals == mx, lane_f, float(LANES)), axis=1, keepdims=True)

    gl = jnp.where(lane < N_GROUPS, logits, ninf)
    gmax, grp = first_max(gl)
    g_gate = 1.0 / jnp.sum(jnp.exp(gl - gmax), axis=1, keepdims=True)
    e_lane = lane - N_GROUPS
    lane_grp = jnp.right_shift(e_lane, 3).astype(F32)
    in_grp = (e_lane >= 0) & (e_lane < N_EXPERTS) & (lane_grp == grp)
    el = jnp.where(in_grp, logits, ninf)
    t1, i1 = first_max(el)
    t2, i2 = first_max(jnp.where(lane_f == i1, ninf, el))
    e21 = jnp.exp(t2 - t1)
    g1 = g_gate / (1.0 + e21)
    g2 = g_gate * e21 / (1.0 + e21)
    idx_ref[...] = jnp.where(lane == 0, i1, jnp.where(lane == 1, i2, float(N_GROUPS))).astype(jnp.int32) - N_GROUPS
    gate_ref[...] = jnp.where(lane == 0, g1, jnp.where(lane == 1, g2, 0.0))


def router(src, g_norm, w_group, w_router, tr):
    m, d = M_ALL, D_MODEL
    w = jnp.pad(jnp.concatenate([w_group, w_router], axis=1).astype(F32), ((0, 0), (0, LANES - N_GROUPS - N_EXPERTS)))
    small = pl.BlockSpec((tr, LANES), lambda i: (i, 0))
    xp, idx, gate = pl.pallas_call(
        functools.partial(_router_body, n_src=len(src)), grid=(m // tr,),
        in_specs=_stream_specs(src, tr) + [pl.BlockSpec((1, d), lambda i: (0, 0)),
                                           pl.BlockSpec((d, LANES), lambda i: (0, 0))],
        out_specs=[pl.BlockSpec((tr * PACK_CHUNKS, LANES), lambda i: (i, 0)), small, small],
        out_shape=[jax.ShapeDtypeStruct((m * PACK_CHUNKS, LANES), jnp.uint32),
                   jax.ShapeDtypeStruct((m, LANES), jnp.int32), jax.ShapeDtypeStruct((m, LANES), F32)],
        compiler_params=_cparams("arbitrary"), name="router")(*src, g_norm.reshape(1, d), w)
    return xp, idx[:, :2], gate


BM = 1024
SUB = 256


def _moe_plan(idx):
    m = idx.shape[0]
    flat_e = idx.reshape(-1)
    onehot = (flat_e[:, None] == jnp.arange(N_EXPERTS, dtype=jnp.int32)[None, :]).astype(jnp.int32)
    csum = jnp.cumsum(onehot, axis=0)
    rank = jnp.take_along_axis(csum, flat_e[:, None], axis=1)[:, 0] - 1
    counts = csum[-1]
    nblk = (counts + BM - 1) // BM
    ends = jnp.cumsum(nblk)
    bstart = ends - nblk
    dest = (bstart[flat_e] * BM + rank).astype(jnp.int32)
    n_blocks = N_EXPERTS + (2 * m) // BM
    n_used = ends[-1]
    used = jnp.arange(n_blocks) < n_used
    blk = jnp.minimum(jnp.arange(n_blocks, dtype=jnp.int32), n_used - 1)
    be = jnp.minimum(jnp.searchsorted(ends, blk, side="right"), N_EXPERTS - 1).astype(jnp.int32)
    cnt = jnp.clip(counts[be] - (blk - bstart[be]) * BM, 0, BM)
    cnt = jnp.where(used, cnt, 0).astype(jnp.int32)
    n_rows = n_blocks * BM
    tok = jnp.arange(2 * m, dtype=jnp.int32) // 2
    row_src = jnp.zeros((n_rows,), jnp.int32).at[dest].set(tok)
    return dict(be=be, bx=blk.astype(jnp.int32), cnt=cnt, row_src=row_src, dest=dest, n_blocks=n_blocks, n_rows=n_rows)


def _gather_rows(idx_of, first, n, src_ref, dst_of, sem, rs):
    def start(r, carry):
        src_row = pl.multiple_of(idx_of(r) * rs, rs)
        pltpu.make_async_copy(src_ref.at[pl.ds(src_row, rs)], dst_of(r), sem).start()
        return carry

    lax.fori_loop(first, first + n, start, 0, unroll=8)


def _ffn_in_body(be_ref, bx_ref, cnt_ref, src_ref, xp_ref, wg_ref, wu_ref, o_ref, xg_ref, xs_ref, sem):
    del be_ref, bx_ref
    b = pl.program_id(0)
    cnt = cnt_ref[b]
    n_sub = (cnt + SUB - 1) // SUB
    sub_words = SUB * PACK_CHUNKS

    @pl.when((pl.program_id(1) == 0) & (cnt > 0))
    def _():
        for sub in range(BM // SUB):
            @pl.when(sub < n_sub)
            def _():
                _gather_rows(lambda r: src_ref[0, 0, r], sub * SUB, SUB, xp_ref,
                             lambda r: xg_ref.at[pl.ds(pl.multiple_of(r * PACK_CHUNKS, PACK_CHUNKS), PACK_CHUNKS)],
                             sem, PACK_CHUNKS)
        for sub in range(BM // SUB):
            @pl.when(sub < n_sub)
            def _():
                rows = pl.ds(sub * sub_words, sub_words)
                pltpu.make_async_copy(xp_ref.at[rows], xg_ref.at[rows], sem).wait()
        half = D_MODEL // 2
        for sub in range(BM // SUB):
            @pl.when(sub < n_sub)
            def _():
                rows = slice(sub * SUB, (sub + 1) * SUB)
                for c in range(PACK_CHUNKS):
                    w = xg_ref[pl.ds(sub * sub_words + c, SUB, stride=PACK_CHUNKS), :]
                    xs_ref[rows, c * LANES:(c + 1) * LANES] = pltpu.bitcast(w & jnp.uint32(HI_MASK), F32).astype(BF16)
                    xs_ref[rows, half + c * LANES:half + (c + 1) * LANES] = pltpu.bitcast(w << 16, F32).astype(BF16)

    @pl.when(cnt > 0)
    def _():
        wg = wg_ref[0, 0].astype(BF16)
        wu = wu_ref[0, 0].astype(BF16)
        for sub in range(BM // SUB):
            @pl.when(sub * SUB < cnt)
            def _():
                x = xs_ref[sub * SUB:(sub + 1) * SUB, :]
                g = _dot(x, wg)
                u = _dot(x, wu)
                o_ref[sub * SUB:(sub + 1) * SUB, :] = (g * (1.0 / (1.0 + jnp.exp(-g))) * u).astype(o_ref.dtype)


def _ffn_out_body(be_ref, bx_ref, cnt_ref, h_ref, w_ref, o_ref, wb_ref):
    del bx_ref
    b = pl.program_id(1)
    cnt = cnt_ref[b]
    new_w = (b == 0) | (be_ref[b] != be_ref[jnp.maximum(b - 1, 0)])

    @pl.when(new_w)
    def _():
        wb_ref[...] = w_ref[0, 0].astype(BF16)

    n_chunks = o_ref.shape[1]
    o2 = o_ref.reshape(BM * n_chunks, LANES)
    for sub in range(BM // SUB):
        @pl.when(sub * SUB < cnt)
        def _():
            y = _dot(h_ref[sub * SUB:(sub + 1) * SUB, :], wb_ref[...])
            for c in range(n_chunks):
                o2[pl.ds(sub * SUB * n_chunks + c, SUB, stride=n_chunks), :] = y[:, c * LANES:(c + 1) * LANES]


def expert_ffn(xp, plan, w_in, w_out, layer, tn_in=256, tn_out=1024):
    n_rows = plan["n_rows"]
    nb = plan["n_blocks"]
    d = D_MODEL
    nci = D_EXPERT // tn_in

    def col(c, b, cnt):
        return jnp.where(cnt[b] > 0, c, nci - 1)

    h_mid = pl.pallas_call(
        _ffn_in_body,
        grid_spec=pltpu.PrefetchScalarGridSpec(
            num_scalar_prefetch=3, grid=(nb, nci),
            in_specs=[pl.BlockSpec((1, 1, BM), lambda b, c, be, bx, cnt: (bx[b], 0, 0), memory_space=pltpu.SMEM),
                      pl.BlockSpec(memory_space=pl.ANY),
                      pl.BlockSpec((1, 1, d, tn_in), lambda b, c, be, bx, cnt: (layer, be[b], 0, col(c, b, cnt))),
                      pl.BlockSpec((1, 1, d, tn_in), lambda b, c, be, bx, cnt: (layer, be[b], 0, nci + col(c, b, cnt)))],
            out_specs=pl.BlockSpec((BM, tn_in), lambda b, c, be, bx, cnt: (bx[b], col(c, b, cnt))),
            scratch_shapes=[pltpu.VMEM((BM * PACK_CHUNKS, LANES), jnp.uint32), pltpu.VMEM((BM, d), BF16),
                            pltpu.SemaphoreType.DMA(())]),
        out_shape=jax.ShapeDtypeStruct((n_rows, D_EXPERT), BF16),
        compiler_params=_cparams("arbitrary", "arbitrary"), name="ffn_in")(
            plan["be"], plan["bx"], plan["cnt"], plan["row_src"].reshape(nb, 1, BM), xp, w_in, w_in)
    nco = d // tn_out
    oc = tn_out // LANES
    return pl.pallas_call(
        _ffn_out_body,
        grid_spec=pltpu.PrefetchScalarGridSpec(
            num_scalar_prefetch=3, grid=(nco, nb),
            in_specs=[pl.BlockSpec((BM, D_EXPERT), lambda c, b, be, bx, cnt: (bx[b], 0)),
                      pl.BlockSpec((1, 1, D_EXPERT, tn_out), lambda c, b, be, bx, cnt: (layer, be[b], 0, c))],
            out_specs=pl.BlockSpec((BM, oc, LANES), lambda c, b, be, bx, cnt: (bx[b], c, 0)),
            scratch_shapes=[pltpu.VMEM((D_EXPERT, tn_out), BF16)]),
        out_shape=jax.ShapeDtypeStruct((n_rows, ROW_CHUNKS, LANES), F32),
        compiler_params=_cparams("arbitrary", "arbitrary"), name="ffn_out")(
            plan["be"], plan["bx"], plan["cnt"], h_mid, w_out)


def _stream_cols(refs, cols):
    if len(refs) == 1:
        return refs[0][:, cols]
    tr = refs[0].shape[0]
    base = jnp.where(pl.program_id(0) < MP // tr, refs[0][:, cols], refs[1][:, cols])
    return base + refs[2][:, cols] if len(refs) == 3 else base


def _combine_body(dcur_ref, dnext_ref, *refs, n_src, n_norm, split_out, n_tiles):
    src_refs = refs[:n_src]
    ys_ref, gate_ref = refs[n_src:n_src + 2]
    rest = refs[n_src + 2:]
    g_ref = None
    if n_norm:
        g_ref, *rest = rest
    n_out = 2 if split_out else 1
    out_refs = rest[:n_out]
    norm_refs = rest[n_out:n_out + n_norm]
    y_ref, sem = rest[n_out + n_norm:]
    i = pl.program_id(0)
    tr = gate_ref.shape[0]
    tile_rows = tr * ROW_CHUNKS
    buf = jnp.bitwise_and(i, 1)

    def fetch(d_ref, s):
        _gather_rows(lambda e: d_ref[0, 0, e], 0, 2 * tr, ys_ref,
                     lambda e: y_ref.at[s, e & 1, pl.ds(pl.multiple_of((e >> 1) * ROW_CHUNKS, ROW_CHUNKS), ROW_CHUNKS)],
                     sem.at[s], ROW_CHUNKS)

    @pl.when(i == 0)
    def _():
        fetch(dcur_ref, 0)

    @pl.when(i + 1 < n_tiles)
    def _():
        fetch(dnext_ref, 1 - buf)

    for slot in range(2):
        pltpu.make_async_copy(ys_ref.at[pl.ds(0, tile_rows)], y_ref.at[buf, slot], sem.at[buf]).wait()
    g0 = gate_ref[:, 0:1]
    g1 = gate_ref[:, 1:2]

    def emit(o_ref):
        ss = jnp.zeros((tr, 1), F32)
        for c in range(ROW_CHUNKS):
            cols = slice(c * LANES, (c + 1) * LANES)
            moe = (g0 * y_ref[buf, 0, pl.ds(c, tr, stride=ROW_CHUNKS), :]
                   + g1 * y_ref[buf, 1, pl.ds(c, tr, stride=ROW_CHUNKS), :])
            hn = _stream_cols(src_refs, cols) + moe
            o_ref[:, cols] = hn
            ss = ss + jnp.sum(hn * hn, axis=1, keepdims=True)
        if n_norm:
            r = lax.rsqrt(ss * (1.0 / D_MODEL) + RMS_EPS)
            for c in range(ROW_CHUNKS):
                cols = slice(c * LANES, (c + 1) * LANES)
                xr = o_ref[:, cols] * r
                for j, n_ref in enumerate(norm_refs):
                    n_ref[:, cols] = (xr * g_ref[j:j + 1, cols]).astype(n_ref.dtype)

    if split_out:
        pl.when(i < n_tiles - 1)(lambda: emit(out_refs[0]))
        pl.when(i == n_tiles - 1)(lambda: emit(out_refs[1]))
    else:
        emit(out_refs[0])


def combine(src, y_sorted, dest, gate, gains, tr, split_out):
    m, d = M_ALL, D_MODEL
    n_tiles = m // tr
    n_norm = 0 if gains is None else gains.shape[0]
    row = pl.BlockSpec((tr, d), lambda i: (i, 0))
    dest3 = dest.reshape(n_tiles, 1, 2 * tr)
    in_specs = [pl.BlockSpec((1, 1, 2 * tr), lambda i: (i, 0, 0), memory_space=pltpu.SMEM),
                pl.BlockSpec((1, 1, 2 * tr), lambda i: (jnp.minimum(i + 1, n_tiles - 1), 0, 0), memory_space=pltpu.SMEM)]
    in_specs += _stream_specs(src, tr) + [pl.BlockSpec(memory_space=pl.ANY), pl.BlockSpec((tr, LANES), lambda i: (i, 0))]
    args = [dest3, dest3, *src, y_sorted, gate]
    if n_norm:
        in_specs.append(pl.BlockSpec((n_norm, d), lambda i: (0, 0)))
        args.append(gains)
    if split_out:
        assert n_norm == 0 and tr == MS
        out_specs = [pl.BlockSpec((tr, d), lambda i: (jnp.minimum(i, n_tiles - 2), 0)), pl.BlockSpec((tr, d), lambda i: (0, 0))]
        out_shape = [jax.ShapeDtypeStruct((MP, d), F32), jax.ShapeDtypeStruct((MS, d), F32)]
    else:
        out_specs = [row] * (1 + n_norm)
        out_shape = [jax.ShapeDtypeStruct((m, d), F32)] + [jax.ShapeDtypeStruct((m, d), BF16)] * n_norm
    return pl.pallas_call(
        functools.partial(_combine_body, n_src=len(src), n_norm=n_norm, split_out=split_out, n_tiles=n_tiles),
        grid=(n_tiles,), in_specs=in_specs, out_specs=out_specs, out_shape=out_shape,
        scratch_shapes=[pltpu.VMEM((2, 2, tr * ROW_CHUNKS, LANES), F32), pltpu.SemaphoreType.DMA((2,))],
        compiler_params=_cparams("arbitrary"), name="combine")(*args)


def hier_moe_layer(src, g_norm, w_group, w_router, w_in, w_out, layer, next_gains, tr, split_out=False):
    xp, idx, gate = router(src, g_norm, w_group, w_router, tr)
    plan = _moe_plan(idx)
    y_sorted = expert_ffn(xp, plan, w_in, w_out, layer)
    return combine(src, y_sorted.reshape(plan["n_rows"] * ROW_CHUNKS, LANES), plan["dest"], gate, next_gains, tr,
                   split_out)


TM = 640
TR = 128


def kernel(x_prompt, x_sample, state_C, state_n, state_m, cache_k, cache_v, page_table, norm_mix, w_in_a, b_gate_a,
           g_hnorm_a, w_out_a, g_kv, w_kv, g_knorm, w_q_b, g_qnorm_b, b_logit_b, w_out_b, norm_ffn, w_group,
           w_router, w_moe_in, w_moe_out):
    xs = jnp.pad(x_sample, ((0, 0), (0, S_PAD - DEC_SEQ), (0, 0))).reshape(MS, D_MODEL)
    x_in = (x_prompt.reshape(MP, D_MODEL), xs)

    (xn0,) = rms_norm(x_in, norm_mix[0:1], [BF16], TR)
    w_in_t = jnp.swapaxes(w_in_a[0], 0, 1)
    p_main = matmul(xn0, w_in_t, A_MAIN, F32, TM, 512, w_transposed=True)
    gates = mlstm_gates(xn0, w_in_t, b_gate_a[0], TM)
    hg, pc, pn, pm = mlstm(p_main, gates, g_hnorm_a[0], None, 0, BATCH, SEQ, SEQ, 256, 256, MLSTM_SUB)
    hg, sc, sn, sm = mlstm(p_main, gates, g_hnorm_a[0], hg, MP, DEC_BATCH, S_PAD, DEC_SEQ, S_PAD, LANES, LANES,
                           init=(state_C[0], state_n[0], state_m[0]))
    y_mix = matmul(hg, w_out_a[0], D_MODEL, F32, TM, 512)
    h1, xn1, xkv = hier_moe_layer(x_in + (y_mix,), norm_ffn[0], w_group[0], w_router[0], w_moe_in, w_moe_out, 0,
                                  jnp.stack([norm_mix[1], g_kv]), TR)

    kv_all = matmul(xkv, w_kv, 2 * KV_HEADS * HEAD_DIM, F32, TM, 512, mode="headnorm", gain=g_knorm,
                    n_norm_tiles=KV_HEADS * HEAD_DIM // 512)
    q_all = matmul(xn1, w_q_b[0], B_HEADS * HEAD_DIM, BF16, TM, 512, mode="headnorm", gain=g_qnorm_b[0],
                   n_norm_tiles=B_HEADS * HEAD_DIM // 512)
    o_all = attn_prompt(q_all, kv_all, b_logit_b[0])
    o_all = attn_sample(q_all, kv_all, cache_k, cache_v, page_table, b_logit_b[0], o_all)
    h2 = matmul(o_all, w_out_b[0], D_MODEL, F32, TM, 512, mode="resid", resid=h1)
    y_p, y_s = hier_moe_layer((h2,), norm_ffn[1], w_group[1], w_router[1], w_moe_in, w_moe_out, 1, None, TR,
                              split_out=True)

    def sample_rows(a):
        return a[MP:].reshape((DEC_BATCH, S_PAD) + a.shape[1:])[:, :DEC_SEQ]

    kvw = KV_HEADS * HEAD_DIM
    y_prompt = y_p.reshape(BATCH, SEQ, D_MODEL)
    y_sample = y_s.reshape(DEC_BATCH, S_PAD, D_MODEL)[:, :DEC_SEQ]
    prompt_k = kv_all[:MP, :kvw].reshape(BATCH, SEQ, KV_HEADS, HEAD_DIM)
    prompt_v = kv_all[:MP, kvw:].reshape(BATCH, SEQ, KV_HEADS, HEAD_DIM)
    sample_k = sample_rows(kv_all[:, :kvw]).reshape(DEC_BATCH, DEC_SEQ, KV_HEADS, HEAD_DIM)
    sample_v = sample_rows(kv_all[:, kvw:]).reshape(DEC_BATCH, DEC_SEQ, KV_HEADS, HEAD_DIM)
    return (y_prompt, y_sample, pc[None], pn[None], pm[None], prompt_k, prompt_v,
            sc[None], sn[None], sm[None], sample_k, sample_v)
```

```python
import functools

import jax
import jax.numpy as jnp
from jax import lax
from jax.experimental import pallas as pl
from jax.experimental.pallas import tpu as pltpu

F32 = jnp.float32
BF16 = jnp.bfloat16

D_MODEL = 4096
BATCH = 4
SEQ = 2048
DEC_BATCH = 8
DEC_SEQ = 4
PAGE_SIZE = 128
A_HEADS = 8
A_QK_DIM = 256
A_V_DIM = 512
A_QK = A_HEADS * A_QK_DIM
A_VD = A_HEADS * A_V_DIM
A_MAIN = 2 * A_QK + A_VD + D_MODEL
HEAD_DIM = 128
B_HEADS = 32
KV_HEADS = 8
Q_PER_KV = B_HEADS // KV_HEADS
N_GROUPS = 4
EXPERTS_PER_GROUP = 8
N_EXPERTS = N_GROUPS * EXPERTS_PER_GROUP
D_EXPERT = 1024
RMS_EPS = 1e-6

LANES = 128
MP = BATCH * SEQ
S_PAD = 16
MS = DEC_BATCH * S_PAD
M_ALL = MP + MS
NEG_BIG = -1e30
VMEM_LIMIT = 56 * 1024 * 1024

HIGHEST = lax.Precision.HIGHEST


def _cparams(*sem):
    return pltpu.CompilerParams(dimension_semantics=sem, vmem_limit_bytes=VMEM_LIMIT)


def _dot(a, b, precision=None):
    return jnp.dot(a, b, preferred_element_type=F32, precision=precision)


def _dot_nt(a, b):
    return lax.dot_general(a, b, (((1,), (1,)), ((), ())), preferred_element_type=F32)


def _log_sigmoid(x):
    return jnp.minimum(x, 0.0) - jnp.log1p(jnp.exp(-jnp.abs(x)))


def _stream_specs(src, tr):
    row = pl.BlockSpec((tr, D_MODEL), lambda i: (i, 0))
    if len(src) == 1:
        return [row]
    assert tr == MS and MP % tr == 0
    last_prompt = MP // tr - 1
    return [pl.BlockSpec((tr, D_MODEL), lambda i: (jnp.minimum(i, last_prompt), 0)),
            pl.BlockSpec((tr, D_MODEL), lambda i: (0, 0))] + [row] * (len(src) - 2)


def _stream_tile(refs):
    if len(refs) == 1:
        return refs[0][...]
    tr = refs[0].shape[0]
    base = jnp.where(pl.program_id(0) < MP // tr, refs[0][...], refs[1][...])
    return base + refs[2][...] if len(refs) == 3 else base


def _rms_body(*refs, n_src):
    g_ref, *out_refs = refs[n_src:]
    x = _stream_tile(refs[:n_src])
    xr = x * lax.rsqrt(jnp.mean(x * x, axis=-1, keepdims=True) + RMS_EPS)
    for j, o_ref in enumerate(out_refs):
        o_ref[...] = (xr * g_ref[j:j + 1, :]).astype(o_ref.dtype)


def rms_norm(src, gains, out_dtypes, tr):
    n = len(out_dtypes)
    row = pl.BlockSpec((tr, D_MODEL), lambda i: (i, 0))
    return pl.pallas_call(
        functools.partial(_rms_body, n_src=len(src)), grid=(M_ALL // tr,),
        in_specs=_stream_specs(src, tr) + [pl.BlockSpec((n, D_MODEL), lambda i: (0, 0))],
        out_specs=[row] * n,
        out_shape=[jax.ShapeDtypeStruct((M_ALL, D_MODEL), dt) for dt in out_dtypes],
        compiler_params=_cparams("arbitrary"), name="rms_norm")(*src, gains)


def _mm_body(x_ref, w_ref, *rest, mode, n_norm_tiles, w_transposed):
    if mode == "resid":
        r_ref, o_ref, wb_ref = rest
    elif mode == "headnorm":
        g_ref, o_ref, wb_ref = rest
    else:
        o_ref, wb_ref = rest
    j = pl.program_id(0)

    @pl.when(pl.program_id(1) == 0)
    def _():
        wb_ref[...] = w_ref[...].astype(BF16)

    acc = _dot_nt(x_ref[...], wb_ref[...]) if w_transposed else _dot(x_ref[...], wb_ref[...])
    if mode == "plain":
        o_ref[...] = acc.astype(o_ref.dtype)
    elif mode == "resid":
        o_ref[...] = r_ref[...] + acc
    else:
        @pl.when(j < n_norm_tiles)
        def _():
            for c in range(acc.shape[1] // HEAD_DIM):
                a = acc[:, c * HEAD_DIM:(c + 1) * HEAD_DIM]
                y = a * lax.rsqrt(jnp.mean(a * a, axis=-1, keepdims=True) + RMS_EPS) * g_ref[...]
                o_ref[:, c * HEAD_DIM:(c + 1) * HEAD_DIM] = y.astype(o_ref.dtype)

        @pl.when(j >= n_norm_tiles)
        def _():
            o_ref[...] = acc.astype(o_ref.dtype)


def matmul(x, w, n_out, out_dtype, tm, tn, mode="plain", resid=None, gain=None, n_norm_tiles=0, w_transposed=False):
    m, k = x.shape
    w_spec = pl.BlockSpec((tn, k), lambda j, i: (j, 0)) if w_transposed else pl.BlockSpec((k, tn), lambda j, i: (0, j))
    in_specs = [pl.BlockSpec((tm, k), lambda j, i: (i, 0)), w_spec]
    args = [x, w]
    if mode == "resid":
        in_specs.append(pl.BlockSpec((tm, tn), lambda j, i: (i, j)))
        args.append(resid)
    elif mode == "headnorm":
        in_specs.append(pl.BlockSpec((1, HEAD_DIM), lambda j, i: (0, 0)))
        args.append(gain.reshape(1, HEAD_DIM))
    return pl.pallas_call(
        functools.partial(_mm_body, mode=mode, n_norm_tiles=n_norm_tiles, w_transposed=w_transposed),
        grid=(n_out // tn, m // tm),
        in_specs=in_specs,
        out_specs=pl.BlockSpec((tm, tn), lambda j, i: (i, j)),
        out_shape=jax.ShapeDtypeStruct((m, n_out), out_dtype),
        scratch_shapes=[pltpu.VMEM((tn, k) if w_transposed else (k, tn), BF16)],
        compiler_params=_cparams("arbitrary", "arbitrary"), name="matmul_" + mode)(*args)


def _gates_body(x_ref, w_ref, b_ref, o_ref):
    n_gate, k = w_ref.shape
    w = jnp.concatenate([w_ref[...], jnp.zeros((LANES - n_gate, k), F32)], axis=0).astype(BF16)
    pre = _dot_nt(x_ref[...], w) + b_ref[...]
    lane = lax.broadcasted_iota(jnp.int32, pre.shape, 1)
    o_ref[...] = jnp.where(lane < A_HEADS, pre, jnp.where(lane < 2 * A_HEADS, _log_sigmoid(pre), 0.0))


def mlstm_gates(x, w_in_t, b_gate, tm):
    m, k = x.shape
    n_gate = 2 * A_HEADS
    assert A_MAIN % n_gate == 0 and w_in_t.shape[0] - A_MAIN == n_gate
    b = jnp.pad(b_gate.astype(F32), (0, LANES - b_gate.shape[0])).reshape(1, LANES)
    return pl.pallas_call(
        _gates_body, grid=(m // tm,),
        in_specs=[pl.BlockSpec((tm, k), lambda i: (i, 0)), pl.BlockSpec((n_gate, k), lambda i: (A_MAIN // n_gate, 0)),
                  pl.BlockSpec((1, LANES), lambda i: (0, 0))],
        out_specs=pl.BlockSpec((tm, LANES), lambda i: (i, 0)),
        out_shape=jax.ShapeDtypeStruct((m, LANES), F32),
        compiler_params=_cparams("arbitrary"), name="mlstm_gates")(x, w_in_t, b)


MLSTM_SUB = 64
def _mlstm_body(*refs, chunk, lb, sub, t_valid, t_total, has_init):
    if has_init:
        q_ref, k_ref, v_ref, o_ref, ga_ref, gh_ref, c0_ref, n0_ref, m0_ref, hg_ref, c_ref, n_ref, m_ref = refs
    else:
        q_ref, k_ref, v_ref, o_ref, ga_ref, gh_ref, hg_ref, c_ref, n_ref, m_ref = refs
    h = pl.program_id(1)
    c = pl.program_id(2)

    @pl.when(c == 0)
    def _():
        if has_init:
            c_ref[...] = c0_ref[...]
            n_ref[...] = n0_ref[...]
            m_ref[...] = m0_ref[...]
        else:
            c_ref[...] = jnp.zeros_like(c_ref)
            n_ref[...] = jnp.zeros_like(n_ref)
            m_ref[...] = jnp.zeros_like(m_ref)

    def load(ref):
        x = ref[...]
        if lb == chunk:
            return x
        return jnp.concatenate([x, jnp.zeros((chunk - lb, x.shape[1]), x.dtype)], axis=0)

    q = load(q_ref).astype(BF16)
    k_f32 = load(k_ref)
    k = k_f32.astype(BF16)
    v = load(v_ref).astype(BF16)
    ga = load(ga_ref)
    c_state = c_ref[0, 0]
    n_state = n_ref[0, 0]
    m_state = m_ref[0, 0][:, 0:1]
    n_sub = chunk // sub
    sub_shift = sub.bit_length() - 1

    lane = lax.broadcasted_iota(jnp.int32, (chunk, LANES), 1)
    i_col = jnp.sum(jnp.where(lane == h, ga, 0.0), axis=1, keepdims=True)
    f_col = jnp.sum(jnp.where(lane == h + A_HEADS, ga, 0.0), axis=1, keepdims=True)
    if t_valid < t_total:
        t_idx = lax.broadcasted_iota(jnp.int32, (chunk, 1), 0) + c * chunk
        i_col = jnp.where(t_idx < t_valid, i_col, NEG_BIG)
        f_col = jnp.where(t_idx < t_valid, f_col, 0.0)

    row_i = lax.broadcasted_iota(jnp.int32, (chunk, chunk), 0)
    col_i = lax.broadcasted_iota(jnp.int32, (chunk, chunk), 1)
    causal = (col_i <= row_i) & (jnp.right_shift(row_i, sub_shift) == jnp.right_shift(col_i, sub_shift))
    g2 = jnp.where(lane == 0, i_col, jnp.where(lane == 1, f_col, 0.0))
    cum = _dot(causal.astype(F32), g2, precision=HIGHEST)
    g3 = jnp.where(lane == 1, cum, g2)
    g3t = g3.T
    i_row = g3t[0:1, :]
    b_row = g3t[1:2, :]
    b_col = g3[:, 1:2]
    sub_of_row = jnp.right_shift(lax.broadcasted_iota(jnp.int32, (1, chunk), 1), sub_shift)
    sub_of_col = jnp.right_shift(lax.broadcasted_iota(jnp.int32, (chunk, 1), 0), sub_shift)

    m_at = [m_state]
    b_last = []
    for j in range(n_sub):
        bl = b_col[(j + 1) * sub - 1:(j + 1) * sub, :]
        g = bl - b_col[j * sub:(j + 1) * sub, :] + i_col[j * sub:(j + 1) * sub, :]
        b_last.append(bl)
        m_at.append(jnp.maximum(bl + m_at[j], jnp.max(g, axis=0, keepdims=True)))
    m_col = jnp.concatenate([jnp.broadcast_to(m_at[j], (sub, 1)) for j in range(n_sub)], axis=0)

    dmat = jnp.where(causal, b_col - b_row + i_row, NEG_BIG)
    inter = b_col + m_col
    m_t = jnp.maximum(inter, jnp.max(dmat, axis=1, keepdims=True))
    w_intra = jnp.exp(dmat - m_t)
    w_inter = jnp.exp(inter - m_t)
    scale = A_QK_DIM ** -0.5
    s = _dot_nt(q, k) * scale * w_intra
    intra = _dot(s.astype(BF16), v)
    s_sum = jnp.sum(s, axis=1, keepdims=True)

    kf = k.astype(F32)
    kt = k_f32.T
    qc_parts = []
    qn_parts = []
    c_cur = c_state
    n_cur = n_state
    for j in range(n_sub):
        qj = q[j * sub:(j + 1) * sub]
        qc_parts.append(_dot(qj, c_cur.astype(BF16)))
        qn_parts.append(jnp.sum(qj.astype(F32) * n_cur.astype(BF16).astype(F32), axis=1, keepdims=True))
        m_new = m_at[j + 1]
        wk_row = jnp.where(sub_of_row == j, jnp.exp(b_last[j] - b_row + i_row - m_new), 0.0)
        wk_col = jnp.where(sub_of_col == j, jnp.exp(b_last[j] - b_col + i_col - m_new), 0.0)
        decay = jnp.exp(b_last[j] + m_at[j] - m_new)
        c_cur = decay * c_cur + _dot((kt * wk_row).astype(BF16), v)
        n_cur = decay * n_cur + jnp.sum(kf * wk_col.astype(BF16).astype(F32), axis=0, keepdims=True)
    num = w_inter * (jnp.concatenate(qc_parts, axis=0) * scale) + intra
    den = w_inter * (jnp.concatenate(qn_parts, axis=0) * scale) + s_sum
    hh = num / jnp.maximum(jnp.abs(den), jnp.exp(-m_t))

    hn = hh * lax.rsqrt(jnp.mean(hh * hh, axis=1, keepdims=True) + RMS_EPS) * gh_ref[0]
    og = o_ref[...].astype(F32)
    out = hn[0:lb] * (1.0 / (1.0 + jnp.exp(-og)))
    hg_ref[...] = out.astype(hg_ref.dtype)

    c_ref[0, 0] = c_cur
    n_ref[0, 0] = n_cur
    m_ref[0, 0] = jnp.broadcast_to(m_at[n_sub], (1, LANES))


def mlstm(p_main, gates, g_hnorm, hg_prev, row0, n_seq, t_total, t_valid, lb, chunk, sub, init=None):
    m = p_main.shape[0]
    nc = t_total // lb
    rb0 = row0 // lb

    def rows(b, h, c):
        return rb0 + b * nc + c

    qk_blocks = A_QK // A_QK_DIM
    in_specs = [
        pl.BlockSpec((lb, A_QK_DIM), lambda b, h, c: (rows(b, h, c), h)),
        pl.BlockSpec((lb, A_QK_DIM), lambda b, h, c: (rows(b, h, c), qk_blocks + h)),
        pl.BlockSpec((lb, A_V_DIM), lambda b, h, c: (rows(b, h, c), 2 * A_QK // A_V_DIM + h)),
        pl.BlockSpec((lb, A_V_DIM), lambda b, h, c: (rows(b, h, c), (2 * A_QK + A_VD) // A_V_DIM + h)),
        pl.BlockSpec((lb, LANES), lambda b, h, c: (rows(b, h, c), 0)),
        pl.BlockSpec((1, 1, A_V_DIM), lambda b, h, c: (h, 0, 0)),
    ]
    args = [p_main, p_main, p_main, p_main, gates, g_hnorm.reshape(A_HEADS, 1, A_V_DIM)]
    state_specs = [
        pl.BlockSpec((1, 1, A_QK_DIM, A_V_DIM), lambda b, h, c: (b, h, 0, 0)),
        pl.BlockSpec((1, 1, 1, A_QK_DIM), lambda b, h, c: (b, h, 0, 0)),
        pl.BlockSpec((1, 1, 1, LANES), lambda b, h, c: (b, h, 0, 0)),
    ]
    has_init = init is not None
    if has_init:
        c0, n0, m0 = init
        in_specs += state_specs
        args += [c0.astype(F32), n0.astype(F32).reshape(n_seq, A_HEADS, 1, A_QK_DIM),
                 jnp.broadcast_to(m0.astype(F32)[:, :, None, None], (n_seq, A_HEADS, 1, LANES))]
    aliases = {}
    if hg_prev is not None:
        in_specs.append(pl.BlockSpec(memory_space=pl.ANY))
        args.append(hg_prev)
        aliases = {len(args) - 1: 0}

    def body(*refs):
        if hg_prev is not None:
            n_in = len(args)
            refs = refs[:n_in - 1] + refs[n_in:]
        _mlstm_body(*refs, chunk=chunk, lb=lb, sub=sub, t_valid=t_valid, t_total=t_total, has_init=has_init)

    hg, c_out, n_out, m_out = pl.pallas_call(
        body, grid=(n_seq, A_HEADS, nc),
        in_specs=in_specs,
        out_specs=[pl.BlockSpec((lb, A_V_DIM), lambda b, h, c: (rows(b, h, c), h))] + state_specs,
        out_shape=[jax.ShapeDtypeStruct((m, A_VD), BF16),
                   jax.ShapeDtypeStruct((n_seq, A_HEADS, A_QK_DIM, A_V_DIM), F32),
                   jax.ShapeDtypeStruct((n_seq, A_HEADS, 1, A_QK_DIM), F32),
                   jax.ShapeDtypeStruct((n_seq, A_HEADS, 1, LANES), F32)],
        input_output_aliases=aliases,
        compiler_params=_cparams("arbitrary", "arbitrary", "arbitrary"), name="mlstm")(*args)
    return hg, c_out, n_out.reshape(n_seq, A_HEADS, A_QK_DIM), m_out[:, :, 0, 0]


def _sb_block(z, tri, carry, mask):
    l1p = jnp.log(1.0 + jnp.exp(-jnp.abs(z)))
    ls = jnp.minimum(z, 0.0) - l1p
    u = ls - z
    if mask is not None:
        u = jnp.where(mask, u, 0.0)
    u_hi = u.astype(BF16)
    u_lo = (u - u_hi.astype(F32)).astype(BF16)
    within = _dot(u_hi, tri) + _dot(u_lo, tri)
    a = jnp.exp(ls + within + carry)
    if mask is not None:
        a = jnp.where(mask, a, 0.0)
    return a, carry + within[:, 0:1] + u[:, 0:1]


def _strict_tri(tk):
    r = lax.broadcasted_iota(jnp.int32, (tk, tk), 0)
    c = lax.broadcasted_iota(jnp.int32, (tk, tk), 1)
    return (r > c).astype(BF16)


def _attn_prompt_body(q_ref, k_ref, v_ref, b_ref, o_ref, *, tq, tk):
    qi = pl.program_id(2)
    scale = HEAD_DIM ** -0.5
    tri = _strict_tri(tk)
    t_loc = lax.broadcasted_iota(jnp.int32, (tq, tk), 0)
    s_loc = lax.broadcasted_iota(jnp.int32, (tq, tk), 1)
    nkb = tq // tk
    rows = Q_PER_KV * tq
    qs = jnp.concatenate([q_ref[:, g * HEAD_DIM:(g + 1) * HEAD_DIM] for g in range(Q_PER_KV)], axis=0)
    bias = jnp.concatenate([jnp.broadcast_to(b_ref[0, :, g:g + 1], (tq, 1)) for g in range(Q_PER_KV)], axis=0)

    def visit(j, state, diag_off):
        carry, acc = state
        start = pl.multiple_of(j * tk, tk)
        kb = k_ref[pl.ds(start, tk), :].astype(BF16)
        vb = v_ref[pl.ds(start, tk), :].astype(BF16)
        z = _dot_nt(qs, kb) * scale + bias
        mask = None
        if diag_off is not None:
            mask = jnp.concatenate([(s_loc + diag_off * tk) < t_loc] * Q_PER_KV, axis=0)
        a, carry = _sb_block(z, tri, carry, mask)
        return carry, acc + _dot(a.astype(BF16), vb)

    state = (jnp.zeros((rows, 1), F32), jnp.zeros((rows, HEAD_DIM), F32))
    for d in reversed(range(nkb)):
        state = visit(qi * nkb + d, state, d)
    n_before = qi * nkb
    state = lax.fori_loop(0, n_before, lambda it, st: visit(n_before - 1 - it, st, None), state)
    for g in range(Q_PER_KV):
        o_ref[:, g * HEAD_DIM:(g + 1) * HEAD_DIM] = state[1][g * tq:(g + 1) * tq].astype(o_ref.dtype)


def attn_prompt(q_all, kv_all, b_logit, tq=256, tk=128):
    m = q_all.shape[0]
    nq = SEQ // tq
    gw = Q_PER_KV * HEAD_DIM
    bias = b_logit.astype(F32).reshape(KV_HEADS, 1, Q_PER_KV)
    return pl.pallas_call(
        functools.partial(_attn_prompt_body, tq=tq, tk=tk),
        grid=(BATCH, KV_HEADS, nq),
        in_specs=[pl.BlockSpec((tq, gw), lambda b, h, i: (b * nq + i, h)),
                  pl.BlockSpec((SEQ, HEAD_DIM), lambda b, h, i: (b, h)),
                  pl.BlockSpec((SEQ, HEAD_DIM), lambda b, h, i: (b, KV_HEADS + h)),
                  pl.BlockSpec((1, 1, Q_PER_KV), lambda b, h, i: (h, 0, 0))],
        out_specs=pl.BlockSpec((tq, gw), lambda b, h, i: (b * nq + i, h)),
        out_shape=jax.ShapeDtypeStruct((m, B_HEADS * HEAD_DIM), BF16),
        compiler_params=_cparams("arbitrary", "arbitrary", "arbitrary"), name="attn_prompt")(q_all, kv_all, kv_all, bias)


QROWS = 8
PAGES_PER_STEP = 8


def _attn_sample_body(pt_ref, q_ref, kn_ref, vn_ref, *rest, n_steps):
    ck_refs = rest[:PAGES_PER_STEP]
    cv_refs = rest[PAGES_PER_STEP:2 * PAGES_PER_STEP]
    b_ref, oin_ref, o_ref, qs_ref, car_ref, acc_ref = rest[2 * PAGES_PER_STEP:]
    del pt_ref, oin_ref
    p = pl.program_id(1)
    scale = HEAD_DIM ** -0.5
    rows_h = Q_PER_KV * QROWS
    tri = _strict_tri(PAGE_SIZE)

    def process(get_k, get_v, mask, car, acc):
        qs = qs_ref[...]
        zs = [_dot_nt(qs[kh * rows_h:(kh + 1) * rows_h].astype(BF16), get_k(kh).astype(BF16)) for kh in range(KV_HEADS)]
        z = jnp.concatenate(zs, axis=0) * scale + b_ref[...]
        a, car = _sb_block(z, tri, car, mask)
        ab = a.astype(BF16)
        accs = [acc[kh * rows_h:(kh + 1) * rows_h] + _dot(ab[kh * rows_h:(kh + 1) * rows_h], get_v(kh).astype(BF16))
                for kh in range(KV_HEADS)]
        return car, jnp.concatenate(accs, axis=0)

    @pl.when(p == 0)
    def _():
        for hq in range(B_HEADS):
            qs_ref[hq * QROWS:(hq + 1) * QROWS, :] = q_ref[:, hq * HEAD_DIM:(hq + 1) * HEAD_DIM].astype(F32)[0:QROWS]
        pad = jnp.zeros((PAGE_SIZE - S_PAD, HEAD_DIM), F32)
        n_rows = B_HEADS * QROWS
        t_loc = jnp.bitwise_and(lax.broadcasted_iota(jnp.int32, (n_rows, PAGE_SIZE), 0), QROWS - 1)
        s_loc = lax.broadcasted_iota(jnp.int32, (n_rows, PAGE_SIZE), 1)
        car, acc = process(lambda kh: jnp.concatenate([kn_ref[:, kh * HEAD_DIM:(kh + 1) * HEAD_DIM], pad], axis=0),
                           lambda kh: jnp.concatenate([vn_ref[:, kh * HEAD_DIM:(kh + 1) * HEAD_DIM], pad], axis=0),
                           s_loc < t_loc, jnp.zeros((n_rows, 1), F32), jnp.zeros((n_rows, HEAD_DIM), F32))
        car_ref[...] = car
        acc_ref[...] = acc

    car = car_ref[...]
    acc = acc_ref[...]
    for ck_ref, cv_ref in zip(ck_refs, cv_refs):
        car, acc = process(lambda kh: ck_ref[0, pl.ds(kh, PAGE_SIZE, stride=KV_HEADS), :],
                           lambda kh: cv_ref[0, pl.ds(kh, PAGE_SIZE, stride=KV_HEADS), :], None, car, acc)
    car_ref[...] = car
    acc_ref[...] = acc

    @pl.when(p == n_steps - 1)
    def _():
        pad = jnp.zeros((S_PAD - QROWS, HEAD_DIM), F32)
        for hq in range(B_HEADS):
            full = jnp.concatenate([acc[hq * QROWS:(hq + 1) * QROWS, :], pad], axis=0)
            o_ref[:, hq * HEAD_DIM:(hq + 1) * HEAD_DIM] = full.astype(o_ref.dtype)


def attn_sample(q_all, kv_all, cache_k, cache_v, page_table, b_logit, o_prev):
    n_seq, n_pages = page_table.shape
    n_steps = n_pages // PAGES_PER_STEP
    rb0 = MP // S_PAD
    kvw = KV_HEADS * HEAD_DIM
    n_phys = cache_k.shape[0]
    ck = cache_k.reshape(n_phys, PAGE_SIZE * KV_HEADS, HEAD_DIM)
    cv = cache_v.reshape(n_phys, PAGE_SIZE * KV_HEADS, HEAD_DIM)
    bias = jnp.broadcast_to(b_logit.astype(F32)[:, None, None], (B_HEADS, QROWS, 1)).reshape(B_HEADS * QROWS, 1)

    def page(u):
        return lambda s, p, pt: (pt[s * n_pages + n_pages - 1 - (p * PAGES_PER_STEP + u)], 0, 0)

    page_specs = [pl.BlockSpec((1, PAGE_SIZE * KV_HEADS, HEAD_DIM), page(u)) for u in range(PAGES_PER_STEP)]
    grid_spec = pltpu.PrefetchScalarGridSpec(
        num_scalar_prefetch=1, grid=(n_seq, n_steps),
        in_specs=[pl.BlockSpec((S_PAD, B_HEADS * HEAD_DIM), lambda s, p, pt: (rb0 + s, 0)),
                  pl.BlockSpec((S_PAD, kvw), lambda s, p, pt: (rb0 + s, 0)),
                  pl.BlockSpec((S_PAD, kvw), lambda s, p, pt: (rb0 + s, 1))] + page_specs + page_specs + [
                  pl.BlockSpec((B_HEADS * QROWS, 1), lambda s, p, pt: (0, 0)),
                  pl.BlockSpec(memory_space=pl.ANY)],
        out_specs=pl.BlockSpec((S_PAD, B_HEADS * HEAD_DIM), lambda s, p, pt: (rb0 + s, 0)),
        scratch_shapes=[pltpu.VMEM((B_HEADS * QROWS, HEAD_DIM), F32),
                        pltpu.VMEM((B_HEADS * QROWS, 1), F32),
                        pltpu.VMEM((B_HEADS * QROWS, HEAD_DIM), F32)])
    args = [page_table.reshape(-1), q_all, kv_all, kv_all] + [ck] * PAGES_PER_STEP + [cv] * PAGES_PER_STEP + [bias, o_prev]
    return pl.pallas_call(
        functools.partial(_attn_sample_body, n_steps=n_steps),
        grid_spec=grid_spec,
        out_shape=jax.ShapeDtypeStruct(o_prev.shape, o_prev.dtype),
        input_output_aliases={len(args) - 1: 0},
        compiler_params=_cparams("arbitrary", "arbitrary"), name="attn_sample")(*args)


PACK_CHUNKS = D_MODEL // 2 // LANES
ROW_CHUNKS = D_MODEL // LANES
STRIDE_PAD = 4
X_STRIDE = PACK_CHUNKS + STRIDE_PAD
Y_STRIDE = ROW_CHUNKS + STRIDE_PAD
HI_MASK = 0xFFFF0000


def _router_body(*refs, n_src):
    g_ref, w_ref, xp_ref, idx_ref, gate_ref = refs[n_src:]
    x = _stream_tile(refs[:n_src])
    tr = x.shape[0]
    xn = x * lax.rsqrt(jnp.mean(x * x, axis=-1, keepdims=True) + RMS_EPS) * g_ref[...]
    xb = xn.astype(BF16)
    bits = pltpu.bitcast(xb.astype(F32), jnp.uint32)
    packed = bits[:, :D_MODEL // 2] | (bits[:, D_MODEL // 2:] >> 16)
    for c in range(PACK_CHUNKS):
        xp_ref[pl.ds(c, tr, stride=PACK_CHUNKS), :] = packed[:, c * LANES:(c + 1) * LANES]
    logits = _dot(xb, w_ref[...].astype(BF16))
    lane = lax.broadcasted_iota(jnp.int32, logits.shape, 1)
    lane_f = lane.astype(F32)
    ninf = -jnp.inf

    def first_max(vals):
        mx = jnp.max(vals, axis=1, keepdims=True)
        return mx, jnp.min(jnp.where(vals == mx, lane_f, float(LANES)), axis=1, keepdims=True)

    gl = jnp.where(lane < N_GROUPS, logits, ninf)
    gmax, grp = first_max(gl)
    g_gate = 1.0 / jnp.sum(jnp.exp(gl - gmax), axis=1, keepdims=True)
    e_lane = lane - N_GROUPS
    lane_grp = jnp.right_shift(e_lane, 3).astype(F32)
    in_grp = (e_lane >= 0) & (e_lane < N_EXPERTS) & (lane_grp == grp)
    el = jnp.where(in_grp, logits, ninf)
    t1, i1 = first_max(el)
    t2, i2 = first_max(jnp.where(lane_f == i1, ninf, el))
    e21 = jnp.exp(t2 - t1)
    g1 = g_gate / (1.0 + e21)
    g2 = g_gate * e21 / (1.0 + e21)
    idx_ref[...] = jnp.where(lane == 0, i1, jnp.where(lane == 1, i2, float(N_GROUPS))).astype(jnp.int32) - N_GROUPS
    gate_ref[...] = jnp.where(lane == 0, g1, jnp.where(lane == 1, g2, 0.0))


def router(src, g_norm, w_group, w_router, tr):
    m, d = M_ALL, D_MODEL
    w = jnp.pad(jnp.concatenate([w_group, w_router], axis=1).astype(F32), ((0, 0), (0, LANES - N_GROUPS - N_EXPERTS)))
    small = pl.BlockSpec((tr, LANES), lambda i: (i, 0))
    xp, idx, gate = pl.pallas_call(
        functools.partial(_router_body, n_src=len(src)), grid=(m // tr,),
        in_specs=_stream_specs(src, tr) + [pl.BlockSpec((1, d), lambda i: (0, 0)),
                                           pl.BlockSpec((d, LANES), lambda i: (0, 0))],
        out_specs=[pl.BlockSpec((tr * PACK_CHUNKS, LANES), lambda i: (i, 0)), small, small],
        out_shape=[jax.ShapeDtypeStruct((m * PACK_CHUNKS, LANES), jnp.uint32),
                   jax.ShapeDtypeStruct((m, LANES), jnp.int32), jax.ShapeDtypeStruct((m, LANES), F32)],
        compiler_params=_cparams("arbitrary"), name="router")(*src, g_norm.reshape(1, d), w)
    return xp, idx[:, :2], gate


BM = 1024
SUB = 256


def _moe_plan(idx):
    m = idx.shape[0]
    flat_e = idx.reshape(-1)
    onehot = (flat_e[:, None] == jnp.arange(N_EXPERTS, dtype=jnp.int32)[None, :]).astype(jnp.int32)
    csum = jnp.cumsum(onehot, axis=0)
    rank = jnp.take_along_axis(csum, flat_e[:, None], axis=1)[:, 0] - 1
    counts = csum[-1]
    nblk = (counts + BM - 1) // BM
    ends = jnp.cumsum(nblk)
    bstart = ends - nblk
    dest = (bstart[flat_e] * BM + rank).astype(jnp.int32)
    n_blocks = N_EXPERTS + (2 * m) // BM
    n_used = ends[-1]
    used = jnp.arange(n_blocks) < n_used
    blk = jnp.minimum(jnp.arange(n_blocks, dtype=jnp.int32), n_used - 1)
    be = jnp.minimum(jnp.searchsorted(ends, blk, side="right"), N_EXPERTS - 1).astype(jnp.int32)
    cnt = jnp.clip(counts[be] - (blk - bstart[be]) * BM, 0, BM)
    cnt = jnp.where(used, cnt, 0).astype(jnp.int32)
    n_rows = n_blocks * BM
    tok = jnp.arange(2 * m, dtype=jnp.int32) // 2
    row_src = jnp.zeros((n_rows,), jnp.int32).at[dest].set(tok)
    return dict(be=be, bx=blk.astype(jnp.int32), cnt=cnt, row_src=row_src, dest=dest, n_blocks=n_blocks, n_rows=n_rows)


def _gather_rows(idx_of, first, n, src_ref, dst_of, sem, rs):
    def start(r, carry):
        src_row = pl.multiple_of(idx_of(r) * rs, rs)
        pltpu.make_async_copy(src_ref.at[pl.ds(src_row, rs)], dst_of(r), sem).start()
        return carry

    lax.fori_loop(first, first + n, start, 0, unroll=8)


def _ffn_in_body(be_ref, bx_ref, cnt_ref, src_ref, xp_ref, wg_ref, wu_ref, o_ref, xg_ref, xs_ref, sem):
    del be_ref, bx_ref
    b = pl.program_id(0)
    cnt = cnt_ref[b]
    n_sub = (cnt + SUB - 1) // SUB
    sub_words = SUB * PACK_CHUNKS

    @pl.when((pl.program_id(1) == 0) & (cnt > 0))
    def _():
        for sub in range(BM // SUB):
            @pl.when(sub < n_sub)
            def _():
                _gather_rows(lambda r: src_ref[0, 0, r], sub * SUB, SUB, xp_ref,
                             lambda r: xg_ref.at[pl.ds(pl.multiple_of(r * X_STRIDE, STRIDE_PAD), PACK_CHUNKS)],
                             sem, PACK_CHUNKS)
        for sub in range(BM // SUB):
            @pl.when(sub < n_sub)
            def _():
                rows = pl.ds(sub * sub_words, sub_words)
                pltpu.make_async_copy(xp_ref.at[rows], xg_ref.at[rows], sem).wait()
        half = D_MODEL // 2
        for sub in range(BM // SUB):
            @pl.when(sub < n_sub)
            def _():
                rows = slice(sub * SUB, (sub + 1) * SUB)
                for c in range(PACK_CHUNKS):
                    w = xg_ref[pl.ds(sub * SUB * X_STRIDE + c, SUB, stride=X_STRIDE), :]
                    xs_ref[rows, c * LANES:(c + 1) * LANES] = pltpu.bitcast(w & jnp.uint32(HI_MASK), F32).astype(BF16)
                    xs_ref[rows, half + c * LANES:half + (c + 1) * LANES] = pltpu.bitcast(w << 16, F32).astype(BF16)

    @pl.when(cnt > 0)
    def _():
        wg = wg_ref[0, 0].astype(BF16)
        wu = wu_ref[0, 0].astype(BF16)
        for sub in range(BM // SUB):
            @pl.when(sub * SUB < cnt)
            def _():
                x = xs_ref[sub * SUB:(sub + 1) * SUB, :]
                g = _dot(x, wg)
                u = _dot(x, wu)
                o_ref[sub * SUB:(sub + 1) * SUB, :] = (g * (1.0 / (1.0 + jnp.exp(-g))) * u).astype(o_ref.dtype)


def _ffn_out_body(be_ref, bx_ref, cnt_ref, h_ref, w_ref, o_ref, wb_ref):
    del bx_ref
    b = pl.program_id(1)
    cnt = cnt_ref[b]
    new_w = (b == 0) | (be_ref[b] != be_ref[jnp.maximum(b - 1, 0)])

    @pl.when(new_w)
    def _():
        wb_ref[...] = w_ref[0, 0].astype(BF16)

    n_chunks = o_ref.shape[1]
    o2 = o_ref.reshape(BM * n_chunks, LANES)
    for sub in range(BM // SUB):
        @pl.when(sub * SUB < cnt)
        def _():
            y = _dot(h_ref[sub * SUB:(sub + 1) * SUB, :], wb_ref[...])
            for c in range(n_chunks):
                o2[pl.ds(sub * SUB * n_chunks + c, SUB, stride=n_chunks), :] = y[:, c * LANES:(c + 1) * LANES]


def expert_ffn(xp, plan, w_in, w_out, layer, tn_in=256, tn_out=1024):
    n_rows = plan["n_rows"]
    nb = plan["n_blocks"]
    d = D_MODEL
    nci = D_EXPERT // tn_in

    def col(c, b, cnt):
        return jnp.where(cnt[b] > 0, c, nci - 1)

    h_mid = pl.pallas_call(
        _ffn_in_body,
        grid_spec=pltpu.PrefetchScalarGridSpec(
            num_scalar_prefetch=3, grid=(nb, nci),
            in_specs=[pl.BlockSpec((1, 1, BM), lambda b, c, be, bx, cnt: (bx[b], 0, 0), memory_space=pltpu.SMEM),
                      pl.BlockSpec(memory_space=pl.ANY),
                      pl.BlockSpec((1, 1, d, tn_in), lambda b, c, be, bx, cnt: (layer, be[b], 0, col(c, b, cnt))),
                      pl.BlockSpec((1, 1, d, tn_in), lambda b, c, be, bx, cnt: (layer, be[b], 0, nci + col(c, b, cnt)))],
            out_specs=pl.BlockSpec((BM, tn_in), lambda b, c, be, bx, cnt: (bx[b], col(c, b, cnt))),
            scratch_shapes=[pltpu.VMEM((BM * X_STRIDE, LANES), jnp.uint32), pltpu.VMEM((BM, d), BF16),
                            pltpu.SemaphoreType.DMA(())]),
        out_shape=jax.ShapeDtypeStruct((n_rows, D_EXPERT), BF16),
        compiler_params=_cparams("arbitrary", "arbitrary"), name="ffn_in")(
            plan["be"], plan["bx"], plan["cnt"], plan["row_src"].reshape(nb, 1, BM), xp, w_in, w_in)
    nco = d // tn_out
    oc = tn_out // LANES
    return pl.pallas_call(
        _ffn_out_body,
        grid_spec=pltpu.PrefetchScalarGridSpec(
            num_scalar_prefetch=3, grid=(nco, nb),
            in_specs=[pl.BlockSpec((BM, D_EXPERT), lambda c, b, be, bx, cnt: (bx[b], 0)),
                      pl.BlockSpec((1, 1, D_EXPERT, tn_out), lambda c, b, be, bx, cnt: (layer, be[b], 0, c))],
            out_specs=pl.BlockSpec((BM, oc, LANES), lambda c, b, be, bx, cnt: (bx[b], c, 0)),
            scratch_shapes=[pltpu.VMEM((D_EXPERT, tn_out), BF16)]),
        out_shape=jax.ShapeDtypeStruct((n_rows, ROW_CHUNKS, LANES), F32),
        compiler_params=_cparams("arbitrary", "arbitrary"), name="ffn_out")(
            plan["be"], plan["bx"], plan["cnt"], h_mid, w_out)


def _stream_cols(refs, cols):
    if len(refs) == 1:
        return refs[0][:, cols]
    tr = refs[0].shape[0]
    base = jnp.where(pl.program_id(0) < MP // tr, refs[0][:, cols], refs[1][:, cols])
    return base + refs[2][:, cols] if len(refs) == 3 else base


def _combine_body(dcur_ref, dnext_ref, *refs, n_src, n_norm, split_out, n_tiles):
    src_refs = refs[:n_src]
    ys_ref, gate_ref = refs[n_src:n_src + 2]
    rest = refs[n_src + 2:]
    g_ref = None
    if n_norm:
        g_ref, *rest = rest
    n_out = 2 if split_out else 1
    out_refs = rest[:n_out]
    norm_refs = rest[n_out:n_out + n_norm]
    y_ref, sem = rest[n_out + n_norm:]
    i = pl.program_id(0)
    tr = gate_ref.shape[0]
    tile_rows = tr * ROW_CHUNKS
    buf = jnp.bitwise_and(i, 1)

    def fetch(d_ref, s):
        _gather_rows(lambda e: d_ref[0, 0, e], 0, 2 * tr, ys_ref,
                     lambda e: y_ref.at[s, e & 1, pl.ds(pl.multiple_of((e >> 1) * Y_STRIDE, STRIDE_PAD), ROW_CHUNKS)],
                     sem.at[s], ROW_CHUNKS)

    @pl.when(i == 0)
    def _():
        fetch(dcur_ref, 0)

    @pl.when(i + 1 < n_tiles)
    def _():
        fetch(dnext_ref, 1 - buf)

    for slot in range(2):
        pltpu.make_async_copy(ys_ref.at[pl.ds(0, tile_rows)], y_ref.at[buf, slot, pl.ds(0, tile_rows)], sem.at[buf]).wait()
    g0 = gate_ref[:, 0:1]
    g1 = gate_ref[:, 1:2]

    def emit(o_ref):
        ss = jnp.zeros((tr, 1), F32)
        for c in range(ROW_CHUNKS):
            cols = slice(c * LANES, (c + 1) * LANES)
            moe = (g0 * y_ref[buf, 0, pl.ds(c, tr, stride=Y_STRIDE), :]
                   + g1 * y_ref[buf, 1, pl.ds(c, tr, stride=Y_STRIDE), :])
            hn = _stream_cols(src_refs, cols) + moe
            o_ref[:, cols] = hn
            ss = ss + jnp.sum(hn * hn, axis=1, keepdims=True)
        if n_norm:
            r = lax.rsqrt(ss * (1.0 / D_MODEL) + RMS_EPS)
            for c in range(ROW_CHUNKS):
                cols = slice(c * LANES, (c + 1) * LANES)
                xr = o_ref[:, cols] * r
                for j, n_ref in enumerate(norm_refs):
                    n_ref[:, cols] = (xr * g_ref[j:j + 1, cols]).astype(n_ref.dtype)

    if split_out:
        pl.when(i < n_tiles - 1)(lambda: emit(out_refs[0]))
        pl.when(i == n_tiles - 1)(lambda: emit(out_refs[1]))
    else:
        emit(out_refs[0])


def combine(src, y_sorted, dest, gate, gains, tr, split_out):
    m, d = M_ALL, D_MODEL
    n_tiles = m // tr
    n_norm = 0 if gains is None else gains.shape[0]
    row = pl.BlockSpec((tr, d), lambda i: (i, 0))
    dest3 = dest.reshape(n_tiles, 1, 2 * tr)
    in_specs = [pl.BlockSpec((1, 1, 2 * tr), lambda i: (i, 0, 0), memory_space=pltpu.SMEM),
                pl.BlockSpec((1, 1, 2 * tr), lambda i: (jnp.minimum(i + 1, n_tiles - 1), 0, 0), memory_space=pltpu.SMEM)]
    in_specs += _stream_specs(src, tr) + [pl.BlockSpec(memory_space=pl.ANY), pl.BlockSpec((tr, LANES), lambda i: (i, 0))]
    args = [dest3, dest3, *src, y_sorted, gate]
    if n_norm:
        in_specs.append(pl.BlockSpec((n_norm, d), lambda i: (0, 0)))
        args.append(gains)
    if split_out:
        assert n_norm == 0 and tr == MS
        out_specs = [pl.BlockSpec((tr, d), lambda i: (jnp.minimum(i, n_tiles - 2), 0)), pl.BlockSpec((tr, d), lambda i: (0, 0))]
        out_shape = [jax.ShapeDtypeStruct((MP, d), F32), jax.ShapeDtypeStruct((MS, d), F32)]
    else:
        out_specs = [row] * (1 + n_norm)
        out_shape = [jax.ShapeDtypeStruct((m, d), F32)] + [jax.ShapeDtypeStruct((m, d), BF16)] * n_norm
    return pl.pallas_call(
        functools.partial(_combine_body, n_src=len(src), n_norm=n_norm, split_out=split_out, n_tiles=n_tiles),
        grid=(n_tiles,), in_specs=in_specs, out_specs=out_specs, out_shape=out_shape,
        scratch_shapes=[pltpu.VMEM((2, 2, tr * Y_STRIDE, LANES), F32), pltpu.SemaphoreType.DMA((2,))],
        compiler_params=_cparams("arbitrary"), name="combine")(*args)


def hier_moe_layer(src, g_norm, w_group, w_router, w_in, w_out, layer, next_gains, tr, split_out=False):
    xp, idx, gate = router(src, g_norm, w_group, w_router, tr)
    plan = _moe_plan(idx)
    y_sorted = expert_ffn(xp, plan, w_in, w_out, layer)
    return combine(src, y_sorted.reshape(plan["n_rows"] * ROW_CHUNKS, LANES), plan["dest"], gate, next_gains, tr,
                   split_out)


TM = 640
TR = 128


def kernel(x_prompt, x_sample, state_C, state_n, state_m, cache_k, cache_v, page_table, norm_mix, w_in_a, b_gate_a,
           g_hnorm_a, w_out_a, g_kv, w_kv, g_knorm, w_q_b, g_qnorm_b, b_logit_b, w_out_b, norm_ffn, w_group,
           w_router, w_moe_in, w_moe_out):
    xs = jnp.pad(x_sample, ((0, 0), (0, S_PAD - DEC_SEQ), (0, 0))).reshape(MS, D_MODEL)
    x_in = (x_prompt.reshape(MP, D_MODEL), xs)

    (xn0,) = rms_norm(x_in, norm_mix[0:1], [BF16], TR)
    w_in_t = jnp.swapaxes(w_in_a[0], 0, 1)
    p_main = matmul(xn0, w_in_t, A_MAIN, F32, TM, 512, w_transposed=True)
    gates = mlstm_gates(xn0, w_in_t, b_gate_a[0], TM)
    hg, pc, pn, pm = mlstm(p_main, gates, g_hnorm_a[0], None, 0, BATCH, SEQ, SEQ, 256, 256, MLSTM_SUB)
    hg, sc, sn, sm = mlstm(p_main, gates, g_hnorm_a[0], hg, MP, DEC_BATCH, S_PAD, DEC_SEQ, S_PAD, LANES, LANES,
                           init=(state_C[0], state_n[0], state_m[0]))
    y_mix = matmul(hg, w_out_a[0], D_MODEL, F32, TM, 512)
    h1, xn1, xkv = hier_moe_layer(x_in + (y_mix,), norm_ffn[0], w_group[0], w_router[0], w_moe_in, w_moe_out, 0,
                                  jnp.stack([norm_mix[1], g_kv]), TR)

    kv_all = matmul(xkv, w_kv, 2 * KV_HEADS * HEAD_DIM, F32, TM, 512, mode="headnorm", gain=g_knorm,
                    n_norm_tiles=KV_HEADS * HEAD_DIM // 512)
    q_all = matmul(xn1, w_q_b[0], B_HEADS * HEAD_DIM, BF16, TM, 512, mode="headnorm", gain=g_qnorm_b[0],
                   n_norm_tiles=B_HEADS * HEAD_DIM // 512)
    o_all = attn_prompt(q_all, kv_all, b_logit_b[0])
    o_all = attn_sample(q_all, kv_all, cache_k, cache_v, page_table, b_logit_b[0], o_all)
    h2 = matmul(o_all, w_out_b[0], D_MODEL, F32, TM, 512, mode="resid", resid=h1)
    y_p, y_s = hier_moe_layer((h2,), norm_ffn[1], w_group[1], w_router[1], w_moe_in, w_moe_out, 1, None, TR,
                              split_out=True)

    def sample_rows(a):
        return a[MP:].reshape((DEC_BATCH, S_PAD) + a.shape[1:])[:, :DEC_SEQ]

    kvw = KV_HEADS * HEAD_DIM
    y_prompt = y_p.reshape(BATCH, SEQ, D_MODEL)
    y_sample = y_s.reshape(DEC_BATCH, S_PAD, D_MODEL)[:, :DEC_SEQ]
    prompt_k = kv_all[:MP, :kvw].reshape(BATCH, SEQ, KV_HEADS, HEAD_DIM)
    prompt_v = kv_all[:MP, kvw:].reshape(BATCH, SEQ, KV_HEADS, HEAD_DIM)
    sample_k = sample_rows(kv_all[:, :kvw]).reshape(DEC_BATCH, DEC_SEQ, KV_HEADS, HEAD_DIM)
    sample_v = sample_rows(kv_all[:, kvw:]).reshape(DEC_BATCH, DEC_SEQ, KV_HEADS, HEAD_DIM)
    return (y_prompt, y_sample, pc[None], pn[None], pm[None], prompt_k, prompt_v,
            sc[None], sn[None], sm[None], sample_k, sample_v)
```

```python
import functools

import jax
import jax.numpy as jnp
from jax import lax
from jax.experimental import pallas as pl
from jax.experimental.pallas import tpu as pltpu

F32 = jnp.float32
BF16 = jnp.bfloat16

D_MODEL = 4096
BATCH = 4
SEQ = 2048
DEC_BATCH = 8
DEC_SEQ = 4
PAGE_SIZE = 128
A_HEADS = 8
A_QK_DIM = 256
A_V_DIM = 512
A_QK = A_HEADS * A_QK_DIM
A_VD = A_HEADS * A_V_DIM
A_MAIN = 2 * A_QK + A_VD + D_MODEL
HEAD_DIM = 128
B_HEADS = 32
KV_HEADS = 8
Q_PER_KV = B_HEADS // KV_HEADS
N_GROUPS = 4
EXPERTS_PER_GROUP = 8
N_EXPERTS = N_GROUPS * EXPERTS_PER_GROUP
D_EXPERT = 1024
RMS_EPS = 1e-6

LANES = 128
MP = BATCH * SEQ
S_PAD = 16
MS = DEC_BATCH * S_PAD
M_ALL = MP + MS
NEG_BIG = -1e30
VMEM_LIMIT = 56 * 1024 * 1024

HIGHEST = lax.Precision.HIGHEST


def _cparams(*sem):
    return pltpu.CompilerParams(dimension_semantics=sem, vmem_limit_bytes=VMEM_LIMIT)


def _dot(a, b, precision=None):
    return jnp.dot(a, b, preferred_element_type=F32, precision=precision)


def _dot_nt(a, b):
    return lax.dot_general(a, b, (((1,), (1,)), ((), ())), preferred_element_type=F32)


def _log_sigmoid(x):
    return jnp.minimum(x, 0.0) - jnp.log1p(jnp.exp(-jnp.abs(x)))


def _stream_specs(src, tr):
    row = pl.BlockSpec((tr, D_MODEL), lambda i: (i, 0))
    if len(src) == 1:
        return [row]
    assert tr == MS and MP % tr == 0
    last_prompt = MP // tr - 1
    return [pl.BlockSpec((tr, D_MODEL), lambda i: (jnp.minimum(i, last_prompt), 0)),
            pl.BlockSpec((tr, D_MODEL), lambda i: (0, 0))] + [row] * (len(src) - 2)


def _stream_tile(refs):
    if len(refs) == 1:
        return refs[0][...]
    tr = refs[0].shape[0]
    base = jnp.where(pl.program_id(0) < MP // tr, refs[0][...], refs[1][...])
    return base + refs[2][...] if len(refs) == 3 else base


def _rms_body(*refs, n_src):
    g_ref, *out_refs = refs[n_src:]
    x = _stream_tile(refs[:n_src])
    xr = x * lax.rsqrt(jnp.mean(x * x, axis=-1, keepdims=True) + RMS_EPS)
    for j, o_ref in enumerate(out_refs):
        o_ref[...] = (xr * g_ref[j:j + 1, :]).astype(o_ref.dtype)


def rms_norm(src, gains, out_dtypes, tr):
    n = len(out_dtypes)
    row = pl.BlockSpec((tr, D_MODEL), lambda i: (i, 0))
    return pl.pallas_call(
        functools.partial(_rms_body, n_src=len(src)), grid=(M_ALL // tr,),
        in_specs=_stream_specs(src, tr) + [pl.BlockSpec((n, D_MODEL), lambda i: (0, 0))],
        out_specs=[row] * n,
        out_shape=[jax.ShapeDtypeStruct((M_ALL, D_MODEL), dt) for dt in out_dtypes],
        compiler_params=_cparams("arbitrary"), name="rms_norm")(*src, gains)


def _mm_body(x_ref, w_ref, *rest, mode, n_norm_tiles, w_transposed):
    if mode == "resid":
        r_ref, o_ref, wb_ref = rest
    elif mode == "headnorm":
        g_ref, o_ref, wb_ref = rest
    else:
        o_ref, wb_ref = rest
    j = pl.program_id(0)

    @pl.when(pl.program_id(1) == 0)
    def _():
        wb_ref[...] = w_ref[...].astype(BF16)

    acc = _dot_nt(x_ref[...], wb_ref[...]) if w_transposed else _dot(x_ref[...], wb_ref[...])
    if mode == "plain":
        o_ref[...] = acc.astype(o_ref.dtype)
    elif mode == "resid":
        o_ref[...] = r_ref[...] + acc
    else:
        @pl.when(j < n_norm_tiles)
        def _():
            for c in range(acc.shape[1] // HEAD_DIM):
                a = acc[:, c * HEAD_DIM:(c + 1) * HEAD_DIM]
                y = a * lax.rsqrt(jnp.mean(a * a, axis=-1, keepdims=True) + RMS_EPS) * g_ref[...]
                o_ref[:, c * HEAD_DIM:(c + 1) * HEAD_DIM] = y.astype(o_ref.dtype)

        @pl.when(j >= n_norm_tiles)
        def _():
            o_ref[...] = acc.astype(o_ref.dtype)


def matmul(x, w, n_out, out_dtype, tm, tn, mode="plain", resid=None, gain=None, n_norm_tiles=0, w_transposed=False):
    m, k = x.shape
    w_spec = pl.BlockSpec((tn, k), lambda j, i: (j, 0)) if w_transposed else pl.BlockSpec((k, tn), lambda j, i: (0, j))
    in_specs = [pl.BlockSpec((tm, k), lambda j, i: (i, 0)), w_spec]
    args = [x, w]
    if mode == "resid":
        in_specs.append(pl.BlockSpec((tm, tn), lambda j, i: (i, j)))
        args.append(resid)
    elif mode == "headnorm":
        in_specs.append(pl.BlockSpec((1, HEAD_DIM), lambda j, i: (0, 0)))
        args.append(gain.reshape(1, HEAD_DIM))
    return pl.pallas_call(
        functools.partial(_mm_body, mode=mode, n_norm_tiles=n_norm_tiles, w_transposed=w_transposed),
        grid=(n_out // tn, m // tm),
        in_specs=in_specs,
        out_specs=pl.BlockSpec((tm, tn), lambda j, i: (i, j)),
        out_shape=jax.ShapeDtypeStruct((m, n_out), out_dtype),
        scratch_shapes=[pltpu.VMEM((tn, k) if w_transposed else (k, tn), BF16)],
        compiler_params=_cparams("arbitrary", "arbitrary"), name="matmul_" + mode)(*args)


def _gates_body(x_ref, w_ref, b_ref, o_ref):
    n_gate, k = w_ref.shape
    w = jnp.concatenate([w_ref[...], jnp.zeros((LANES - n_gate, k), F32)], axis=0).astype(BF16)
    pre = _dot_nt(x_ref[...], w) + b_ref[...]
    lane = lax.broadcasted_iota(jnp.int32, pre.shape, 1)
    o_ref[...] = jnp.where(lane < A_HEADS, pre, jnp.where(lane < 2 * A_HEADS, _log_sigmoid(pre), 0.0))


def mlstm_gates(x, w_in_t, b_gate, tm):
    m, k = x.shape
    n_gate = 2 * A_HEADS
    assert A_MAIN % n_gate == 0 and w_in_t.shape[0] - A_MAIN == n_gate
    b = jnp.pad(b_gate.astype(F32), (0, LANES - b_gate.shape[0])).reshape(1, LANES)
    return pl.pallas_call(
        _gates_body, grid=(m // tm,),
        in_specs=[pl.BlockSpec((tm, k), lambda i: (i, 0)), pl.BlockSpec((n_gate, k), lambda i: (A_MAIN // n_gate, 0)),
                  pl.BlockSpec((1, LANES), lambda i: (0, 0))],
        out_specs=pl.BlockSpec((tm, LANES), lambda i: (i, 0)),
        out_shape=jax.ShapeDtypeStruct((m, LANES), F32),
        compiler_params=_cparams("arbitrary"), name="mlstm_gates")(x, w_in_t, b)


MLSTM_SUB = 64
def _mlstm_body(*refs, chunk, lb, sub, t_valid, t_total, has_init):
    if has_init:
        q_ref, k_ref, v_ref, o_ref, ga_ref, gh_ref, c0_ref, n0_ref, m0_ref, hg_ref, c_ref, n_ref, m_ref = refs
    else:
        q_ref, k_ref, v_ref, o_ref, ga_ref, gh_ref, hg_ref, c_ref, n_ref, m_ref = refs
    h = pl.program_id(1)
    c = pl.program_id(2)

    @pl.when(c == 0)
    def _():
        if has_init:
            c_ref[...] = c0_ref[...]
            n_ref[...] = n0_ref[...]
            m_ref[...] = m0_ref[...]
        else:
            c_ref[...] = jnp.zeros_like(c_ref)
            n_ref[...] = jnp.zeros_like(n_ref)
            m_ref[...] = jnp.zeros_like(m_ref)

    def load(ref):
        x = ref[...]
        if lb == chunk:
            return x
        return jnp.concatenate([x, jnp.zeros((chunk - lb, x.shape[1]), x.dtype)], axis=0)

    q = load(q_ref).astype(BF16)
    k_f32 = load(k_ref)
    k = k_f32.astype(BF16)
    v = load(v_ref).astype(BF16)
    ga = load(ga_ref)
    c_state = c_ref[0, 0]
    n_state = n_ref[0, 0]
    m_state = m_ref[0, 0][:, 0:1]
    n_sub = chunk // sub
    sub_shift = sub.bit_length() - 1

    lane = lax.broadcasted_iota(jnp.int32, (chunk, LANES), 1)
    i_col = jnp.sum(jnp.where(lane == h, ga, 0.0), axis=1, keepdims=True)
    f_col = jnp.sum(jnp.where(lane == h + A_HEADS, ga, 0.0), axis=1, keepdims=True)
    if t_valid < t_total:
        t_idx = lax.broadcasted_iota(jnp.int32, (chunk, 1), 0) + c * chunk
        i_col = jnp.where(t_idx < t_valid, i_col, NEG_BIG)
        f_col = jnp.where(t_idx < t_valid, f_col, 0.0)

    row_i = lax.broadcasted_iota(jnp.int32, (chunk, chunk), 0)
    col_i = lax.broadcasted_iota(jnp.int32, (chunk, chunk), 1)
    causal = (col_i <= row_i) & (jnp.right_shift(row_i, sub_shift) == jnp.right_shift(col_i, sub_shift))
    g2 = jnp.where(lane == 0, i_col, jnp.where(lane == 1, f_col, 0.0))
    cum = _dot(causal.astype(F32), g2, precision=HIGHEST)
    g3 = jnp.where(lane == 1, cum, g2)
    g3t = g3.T
    i_row = g3t[0:1, :]
    b_row = g3t[1:2, :]
    b_col = g3[:, 1:2]
    sub_of_row = jnp.right_shift(lax.broadcasted_iota(jnp.int32, (1, chunk), 1), sub_shift)
    sub_of_col = jnp.right_shift(lax.broadcasted_iota(jnp.int32, (chunk, 1), 0), sub_shift)

    m_at = [m_state]
    b_last = []
    for j in range(n_sub):
        bl = b_col[(j + 1) * sub - 1:(j + 1) * sub, :]
        g = bl - b_col[j * sub:(j + 1) * sub, :] + i_col[j * sub:(j + 1) * sub, :]
        b_last.append(bl)
        m_at.append(jnp.maximum(bl + m_at[j], jnp.max(g, axis=0, keepdims=True)))
    m_col = jnp.concatenate([jnp.broadcast_to(m_at[j], (sub, 1)) for j in range(n_sub)], axis=0)

    dmat = jnp.where(causal, b_col - b_row + i_row, NEG_BIG)
    inter = b_col + m_col
    m_t = jnp.maximum(inter, jnp.max(dmat, axis=1, keepdims=True))
    w_intra = jnp.exp(dmat - m_t)
    w_inter = jnp.exp(inter - m_t)
    scale = A_QK_DIM ** -0.5
    s = _dot_nt(q, k) * scale * w_intra
    intra = _dot(s.astype(BF16), v)
    s_sum = jnp.sum(s, axis=1, keepdims=True)

    kf = k.astype(F32)
    kt = k_f32.T
    qc_parts = []
    qn_parts = []
    c_cur = c_state
    n_cur = n_state
    for j in range(n_sub):
        qj = q[j * sub:(j + 1) * sub]
        qc_parts.append(_dot(qj, c_cur.astype(BF16)))
        qn_parts.append(jnp.sum(qj.astype(F32) * n_cur.astype(BF16).astype(F32), axis=1, keepdims=True))
        m_new = m_at[j + 1]
        wk_row = jnp.where(sub_of_row == j, jnp.exp(b_last[j] - b_row + i_row - m_new), 0.0)
        wk_col = jnp.where(sub_of_col == j, jnp.exp(b_last[j] - b_col + i_col - m_new), 0.0)
        decay = jnp.exp(b_last[j] + m_at[j] - m_new)
        c_cur = decay * c_cur + _dot((kt * wk_row).astype(BF16), v)
        n_cur = decay * n_cur + jnp.sum(kf * wk_col.astype(BF16).astype(F32), axis=0, keepdims=True)
    num = w_inter * (jnp.concatenate(qc_parts, axis=0) * scale) + intra
    den = w_inter * (jnp.concatenate(qn_parts, axis=0) * scale) + s_sum
    hh = num / jnp.maximum(jnp.abs(den), jnp.exp(-m_t))

    hn = hh * lax.rsqrt(jnp.mean(hh * hh, axis=1, keepdims=True) + RMS_EPS) * gh_ref[0]
    og = o_ref[...].astype(F32)
    out = hn[0:lb] * (1.0 / (1.0 + jnp.exp(-og)))
    hg_ref[...] = out.astype(hg_ref.dtype)

    c_ref[0, 0] = c_cur
    n_ref[0, 0] = n_cur
    m_ref[0, 0] = jnp.broadcast_to(m_at[n_sub], (1, LANES))


def mlstm(p_main, gates, g_hnorm, hg_prev, row0, n_seq, t_total, t_valid, lb, chunk, sub, init=None):
    m = p_main.shape[0]
    nc = t_total // lb
    rb0 = row0 // lb

    def rows(b, h, c):
        return rb0 + b * nc + c

    qk_blocks = A_QK // A_QK_DIM
    in_specs = [
        pl.BlockSpec((lb, A_QK_DIM), lambda b, h, c: (rows(b, h, c), h)),
        pl.BlockSpec((lb, A_QK_DIM), lambda b, h, c: (rows(b, h, c), qk_blocks + h)),
        pl.BlockSpec((lb, A_V_DIM), lambda b, h, c: (rows(b, h, c), 2 * A_QK // A_V_DIM + h)),
        pl.BlockSpec((lb, A_V_DIM), lambda b, h, c: (rows(b, h, c), (2 * A_QK + A_VD) // A_V_DIM + h)),
        pl.BlockSpec((lb, LANES), lambda b, h, c: (rows(b, h, c), 0)),
        pl.BlockSpec((1, 1, A_V_DIM), lambda b, h, c: (h, 0, 0)),
    ]
    args = [p_main, p_main, p_main, p_main, gates, g_hnorm.reshape(A_HEADS, 1, A_V_DIM)]
    state_specs = [
        pl.BlockSpec((1, 1, A_QK_DIM, A_V_DIM), lambda b, h, c: (b, h, 0, 0)),
        pl.BlockSpec((1, 1, 1, A_QK_DIM), lambda b, h, c: (b, h, 0, 0)),
        pl.BlockSpec((1, 1, 1, LANES), lambda b, h, c: (b, h, 0, 0)),
    ]
    has_init = init is not None
    if has_init:
        c0, n0, m0 = init
        in_specs += state_specs
        args += [c0.astype(F32), n0.astype(F32).reshape(n_seq, A_HEADS, 1, A_QK_DIM),
                 jnp.broadcast_to(m0.astype(F32)[:, :, None, None], (n_seq, A_HEADS, 1, LANES))]
    aliases = {}
    if hg_prev is not None:
        in_specs.append(pl.BlockSpec(memory_space=pl.ANY))
        args.append(hg_prev)
        aliases = {len(args) - 1: 0}

    def body(*refs):
        if hg_prev is not None:
            n_in = len(args)
            refs = refs[:n_in - 1] + refs[n_in:]
        _mlstm_body(*refs, chunk=chunk, lb=lb, sub=sub, t_valid=t_valid, t_total=t_total, has_init=has_init)

    hg, c_out, n_out, m_out = pl.pallas_call(
        body, grid=(n_seq, A_HEADS, nc),
        in_specs=in_specs,
        out_specs=[pl.BlockSpec((lb, A_V_DIM), lambda b, h, c: (rows(b, h, c), h))] + state_specs,
        out_shape=[jax.ShapeDtypeStruct((m, A_VD), BF16),
                   jax.ShapeDtypeStruct((n_seq, A_HEADS, A_QK_DIM, A_V_DIM), F32),
                   jax.ShapeDtypeStruct((n_seq, A_HEADS, 1, A_QK_DIM), F32),
                   jax.ShapeDtypeStruct((n_seq, A_HEADS, 1, LANES), F32)],
        input_output_aliases=aliases,
        compiler_params=_cparams("arbitrary", "arbitrary", "arbitrary"), name="mlstm")(*args)
    return hg, c_out, n_out.reshape(n_seq, A_HEADS, A_QK_DIM), m_out[:, :, 0, 0]


def _sb_block(z, tri, carry, mask):
    l1p = jnp.log(1.0 + jnp.exp(-jnp.abs(z)))
    ls = jnp.minimum(z, 0.0) - l1p
    u = ls - z
    if mask is not None:
        u = jnp.where(mask, u, 0.0)
    u_hi = u.astype(BF16)
    u_lo = (u - u_hi.astype(F32)).astype(BF16)
    within = _dot(u_hi, tri) + _dot(u_lo, tri)
    a = jnp.exp(ls + within + carry)
    if mask is not None:
        a = jnp.where(mask, a, 0.0)
    return a, carry + within[:, 0:1] + u[:, 0:1]


def _strict_tri(tk):
    r = lax.broadcasted_iota(jnp.int32, (tk, tk), 0)
    c = lax.broadcasted_iota(jnp.int32, (tk, tk), 1)
    return (r > c).astype(BF16)


def _attn_prompt_body(q_ref, k_ref, v_ref, b_ref, o_ref, *, tq, tk):
    qi = pl.program_id(2)
    scale = HEAD_DIM ** -0.5
    tri = _strict_tri(tk)
    t_loc = lax.broadcasted_iota(jnp.int32, (tq, tk), 0)
    s_loc = lax.broadcasted_iota(jnp.int32, (tq, tk), 1)
    nkb = tq // tk
    rows = Q_PER_KV * tq
    qs = jnp.concatenate([q_ref[:, g * HEAD_DIM:(g + 1) * HEAD_DIM] for g in range(Q_PER_KV)], axis=0)
    bias = jnp.concatenate([jnp.broadcast_to(b_ref[0, :, g:g + 1], (tq, 1)) for g in range(Q_PER_KV)], axis=0)

    def visit(j, state, diag_off):
        carry, acc = state
        start = pl.multiple_of(j * tk, tk)
        kb = k_ref[pl.ds(start, tk), :].astype(BF16)
        vb = v_ref[pl.ds(start, tk), :].astype(BF16)
        z = _dot_nt(qs, kb) * scale + bias
        mask = None
        if diag_off is not None:
            mask = jnp.concatenate([(s_loc + diag_off * tk) < t_loc] * Q_PER_KV, axis=0)
        a, carry = _sb_block(z, tri, carry, mask)
        return carry, acc + _dot(a.astype(BF16), vb)

    state = (jnp.zeros((rows, 1), F32), jnp.zeros((rows, HEAD_DIM), F32))
    for d in reversed(range(nkb)):
        state = visit(qi * nkb + d, state, d)
    n_before = qi * nkb
    state = lax.fori_loop(0, n_before, lambda it, st: visit(n_before - 1 - it, st, None), state)
    for g in range(Q_PER_KV):
        o_ref[:, g * HEAD_DIM:(g + 1) * HEAD_DIM] = state[1][g * tq:(g + 1) * tq].astype(o_ref.dtype)


def attn_prompt(q_all, kv_all, b_logit, tq=256, tk=128):
    m = q_all.shape[0]
    nq = SEQ // tq
    gw = Q_PER_KV * HEAD_DIM
    bias = b_logit.astype(F32).reshape(KV_HEADS, 1, Q_PER_KV)
    return pl.pallas_call(
        functools.partial(_attn_prompt_body, tq=tq, tk=tk),
        grid=(BATCH, KV_HEADS, nq),
        in_specs=[pl.BlockSpec((tq, gw), lambda b, h, i: (b * nq + i, h)),
                  pl.BlockSpec((SEQ, HEAD_DIM), lambda b, h, i: (b, h)),
                  pl.BlockSpec((SEQ, HEAD_DIM), lambda b, h, i: (b, KV_HEADS + h)),
                  pl.BlockSpec((1, 1, Q_PER_KV), lambda b, h, i: (h, 0, 0))],
        out_specs=pl.BlockSpec((tq, gw), lambda b, h, i: (b * nq + i, h)),
        out_shape=jax.ShapeDtypeStruct((m, B_HEADS * HEAD_DIM), BF16),
        compiler_params=_cparams("arbitrary", "arbitrary", "arbitrary"), name="attn_prompt")(q_all, kv_all, kv_all, bias)


QROWS = 8
PAGES_PER_STEP = 8


def _attn_sample_body(pt_ref, q_ref, kn_ref, vn_ref, *rest, n_steps):
    ck_refs = rest[:PAGES_PER_STEP]
    cv_refs = rest[PAGES_PER_STEP:2 * PAGES_PER_STEP]
    b_ref, oin_ref, o_ref, qs_ref, car_ref, acc_ref = rest[2 * PAGES_PER_STEP:]
    del pt_ref, oin_ref
    p = pl.program_id(1)
    scale = HEAD_DIM ** -0.5
    rows_h = Q_PER_KV * QROWS
    tri = _strict_tri(PAGE_SIZE)

    def process(get_k, get_v, mask, car, acc):
        qs = qs_ref[...]
        zs = [_dot_nt(qs[kh * rows_h:(kh + 1) * rows_h].astype(BF16), get_k(kh).astype(BF16)) for kh in range(KV_HEADS)]
        z = jnp.concatenate(zs, axis=0) * scale + b_ref[...]
        a, car = _sb_block(z, tri, car, mask)
        ab = a.astype(BF16)
        accs = [acc[kh * rows_h:(kh + 1) * rows_h] + _dot(ab[kh * rows_h:(kh + 1) * rows_h], get_v(kh).astype(BF16))
                for kh in range(KV_HEADS)]
        return car, jnp.concatenate(accs, axis=0)

    @pl.when(p == 0)
    def _():
        for hq in range(B_HEADS):
            qs_ref[hq * QROWS:(hq + 1) * QROWS, :] = q_ref[:, hq * HEAD_DIM:(hq + 1) * HEAD_DIM].astype(F32)[0:QROWS]
        pad = jnp.zeros((PAGE_SIZE - S_PAD, HEAD_DIM), F32)
        n_rows = B_HEADS * QROWS
        t_loc = jnp.bitwise_and(lax.broadcasted_iota(jnp.int32, (n_rows, PAGE_SIZE), 0), QROWS - 1)
        s_loc = lax.broadcasted_iota(jnp.int32, (n_rows, PAGE_SIZE), 1)
        car, acc = process(lambda kh: jnp.concatenate([kn_ref[:, kh * HEAD_DIM:(kh + 1) * HEAD_DIM], pad], axis=0),
                           lambda kh: jnp.concatenate([vn_ref[:, kh * HEAD_DIM:(kh + 1) * HEAD_DIM], pad], axis=0),
                           s_loc < t_loc, jnp.zeros((n_rows, 1), F32), jnp.zeros((n_rows, HEAD_DIM), F32))
        car_ref[...] = car
        acc_ref[...] = acc

    car = car_ref[...]
    acc = acc_ref[...]
    for ck_ref, cv_ref in zip(ck_refs, cv_refs):
        car, acc = process(lambda kh: ck_ref[0, pl.ds(kh, PAGE_SIZE, stride=KV_HEADS), :],
                           lambda kh: cv_ref[0, pl.ds(kh, PAGE_SIZE, stride=KV_HEADS), :], None, car, acc)
    car_ref[...] = car
    acc_ref[...] = acc

    @pl.when(p == n_steps - 1)
    def _():
        pad = jnp.zeros((S_PAD - QROWS, HEAD_DIM), F32)
        for hq in range(B_HEADS):
            full = jnp.concatenate([acc[hq * QROWS:(hq + 1) * QROWS, :], pad], axis=0)
            o_ref[:, hq * HEAD_DIM:(hq + 1) * HEAD_DIM] = full.astype(o_ref.dtype)


def attn_sample(q_all, kv_all, cache_k, cache_v, page_table, b_logit, o_prev):
    n_seq, n_pages = page_table.shape
    n_steps = n_pages // PAGES_PER_STEP
    rb0 = MP // S_PAD
    kvw = KV_HEADS * HEAD_DIM
    n_phys = cache_k.shape[0]
    ck = cache_k.reshape(n_phys, PAGE_SIZE * KV_HEADS, HEAD_DIM)
    cv = cache_v.reshape(n_phys, PAGE_SIZE * KV_HEADS, HEAD_DIM)
    bias = jnp.broadcast_to(b_logit.astype(F32)[:, None, None], (B_HEADS, QROWS, 1)).reshape(B_HEADS * QROWS, 1)

    def page(u):
        return lambda s, p, pt: (pt[s * n_pages + n_pages - 1 - (p * PAGES_PER_STEP + u)], 0, 0)

    page_specs = [pl.BlockSpec((1, PAGE_SIZE * KV_HEADS, HEAD_DIM), page(u)) for u in range(PAGES_PER_STEP)]
    grid_spec = pltpu.PrefetchScalarGridSpec(
        num_scalar_prefetch=1, grid=(n_seq, n_steps),
        in_specs=[pl.BlockSpec((S_PAD, B_HEADS * HEAD_DIM), lambda s, p, pt: (rb0 + s, 0)),
                  pl.BlockSpec((S_PAD, kvw), lambda s, p, pt: (rb0 + s, 0)),
                  pl.BlockSpec((S_PAD, kvw), lambda s, p, pt: (rb0 + s, 1))] + page_specs + page_specs + [
                  pl.BlockSpec((B_HEADS * QROWS, 1), lambda s, p, pt: (0, 0)),
                  pl.BlockSpec(memory_space=pl.ANY)],
        out_specs=pl.BlockSpec((S_PAD, B_HEADS * HEAD_DIM), lambda s, p, pt: (rb0 + s, 0)),
        scratch_shapes=[pltpu.VMEM((B_HEADS * QROWS, HEAD_DIM), F32),
                        pltpu.VMEM((B_HEADS * QROWS, 1), F32),
                        pltpu.VMEM((B_HEADS * QROWS, HEAD_DIM), F32)])
    args = [page_table.reshape(-1), q_all, kv_all, kv_all] + [ck] * PAGES_PER_STEP + [cv] * PAGES_PER_STEP + [bias, o_prev]
    return pl.pallas_call(
        functools.partial(_attn_sample_body, n_steps=n_steps),
        grid_spec=grid_spec,
        out_shape=jax.ShapeDtypeStruct(o_prev.shape, o_prev.dtype),
        input_output_aliases={len(args) - 1: 0},
        compiler_params=_cparams("arbitrary", "arbitrary"), name="attn_sample")(*args)


PACK_CHUNKS = D_MODEL // 2 // LANES
ROW_CHUNKS = D_MODEL // LANES
STRIDE_PAD = 4
X_STRIDE = PACK_CHUNKS + STRIDE_PAD
Y_STRIDE = ROW_CHUNKS + STRIDE_PAD
HI_MASK = 0xFFFF0000


def _router_body(*refs, n_src):
    g_ref, w_ref, xp_ref, idx_ref, gate_ref = refs[n_src:]
    x = _stream_tile(refs[:n_src])
    tr = x.shape[0]
    xn = x * lax.rsqrt(jnp.mean(x * x, axis=-1, keepdims=True) + RMS_EPS) * g_ref[...]
    xb = xn.astype(BF16)
    bits = pltpu.bitcast(xb.astype(F32), jnp.uint32)
    packed = bits[:, :D_MODEL // 2] | (bits[:, D_MODEL // 2:] >> 16)
    for c in range(PACK_CHUNKS):
        xp_ref[pl.ds(c, tr, stride=PACK_CHUNKS), :] = packed[:, c * LANES:(c + 1) * LANES]
    logits = _dot(xb, w_ref[...].astype(BF16))
    lane = lax.broadcasted_iota(jnp.int32, logits.shape, 1)
    lane_f = lane.astype(F32)
    ninf = -jnp.inf

    def first_max(vals):
        mx = jnp.max(vals, axis=1, keepdims=True)
        return mx, jnp.min(jnp.where(vals == mx, lane_f, float(LANES)), axis=1, keepdims=True)

    gl = jnp.where(lane < N_GROUPS, logits, ninf)
    gmax, grp = first_max(gl)
    g_gate = 1.0 / jnp.sum(jnp.exp(gl - gmax), axis=1, keepdims=True)
    e_lane = lane - N_GROUPS
    lane_grp = jnp.right_shift(e_lane, 3).astype(F32)
    in_grp = (e_lane >= 0) & (e_lane < N_EXPERTS) & (lane_grp == grp)
    el = jnp.where(in_grp, logits, ninf)
    t1, i1 = first_max(el)
    t2, i2 = first_max(jnp.where(lane_f == i1, ninf, el))
    e21 = jnp.exp(t2 - t1)
    g1 = g_gate / (1.0 + e21)
    g2 = g_gate * e21 / (1.0 + e21)
    idx_ref[...] = jnp.where(lane == 0, i1, jnp.where(lane == 1, i2, float(N_GROUPS))).astype(jnp.int32) - N_GROUPS
    gate_ref[...] = jnp.where(lane == 0, g1, jnp.where(lane == 1, g2, 0.0))


def router(src, g_norm, w_group, w_router, tr):
    m, d = M_ALL, D_MODEL
    w = jnp.pad(jnp.concatenate([w_group, w_router], axis=1).astype(F32), ((0, 0), (0, LANES - N_GROUPS - N_EXPERTS)))
    small = pl.BlockSpec((tr, LANES), lambda i: (i, 0))
    xp, idx, gate = pl.pallas_call(
        functools.partial(_router_body, n_src=len(src)), grid=(m // tr,),
        in_specs=_stream_specs(src, tr) + [pl.BlockSpec((1, d), lambda i: (0, 0)),
                                           pl.BlockSpec((d, LANES), lambda i: (0, 0))],
        out_specs=[pl.BlockSpec((tr * PACK_CHUNKS, LANES), lambda i: (i, 0)), small, small],
        out_shape=[jax.ShapeDtypeStruct((m * PACK_CHUNKS, LANES), jnp.uint32),
                   jax.ShapeDtypeStruct((m, LANES), jnp.int32), jax.ShapeDtypeStruct((m, LANES), F32)],
        compiler_params=_cparams("arbitrary"), name="router")(*src, g_norm.reshape(1, d), w)
    return xp, idx[:, :2], gate


BM = 1024
SUB = 256


def _moe_plan(idx):
    m = idx.shape[0]
    flat_e = idx.reshape(-1)
    onehot = (flat_e[:, None] == jnp.arange(N_EXPERTS, dtype=jnp.int32)[None, :]).astype(jnp.int32)
    csum = jnp.cumsum(onehot, axis=0)
    rank = jnp.take_along_axis(csum, flat_e[:, None], axis=1)[:, 0] - 1
    counts = csum[-1]
    nblk = (counts + BM - 1) // BM
    ends = jnp.cumsum(nblk)
    bstart = ends - nblk
    dest = (bstart[flat_e] * BM + rank).astype(jnp.int32)
    n_blocks = N_EXPERTS + (2 * m) // BM
    n_used = ends[-1]
    used = jnp.arange(n_blocks) < n_used
    blk = jnp.minimum(jnp.arange(n_blocks, dtype=jnp.int32), n_used - 1)
    be = jnp.minimum(jnp.searchsorted(ends, blk, side="right"), N_EXPERTS - 1).astype(jnp.int32)
    cnt = jnp.clip(counts[be] - (blk - bstart[be]) * BM, 0, BM)
    cnt = jnp.where(used, cnt, 0).astype(jnp.int32)
    n_rows = n_blocks * BM
    tok = jnp.arange(2 * m, dtype=jnp.int32) // 2
    row_src = jnp.zeros((n_rows,), jnp.int32).at[dest].set(tok)
    return dict(be=be, bx=blk.astype(jnp.int32), cnt=cnt, row_src=row_src, dest=dest, n_blocks=n_blocks, n_rows=n_rows)


def _gather_rows(idx_of, first, n, src_ref, dst_of, sem, rs):
    def start(r, carry):
        src_row = pl.multiple_of(idx_of(r) * rs, rs)
        pltpu.make_async_copy(src_ref.at[pl.ds(src_row, rs)], dst_of(r), sem).start()
        return carry

    lax.fori_loop(first, first + n, start, 0, unroll=8)


def _ffn_in_body(be_ref, bx_ref, cnt_ref, src_ref, xp_ref, wg_ref, wu_ref, o_ref, xg_ref, xs_ref, sem):
    del be_ref, bx_ref
    b = pl.program_id(0)
    cnt = cnt_ref[b]
    n_sub = (cnt + SUB - 1) // SUB
    sub_words = SUB * PACK_CHUNKS

    @pl.when((pl.program_id(1) == 0) & (cnt > 0))
    def _():
        for sub in range(BM // SUB):
            @pl.when(sub < n_sub)
            def _():
                _gather_rows(lambda r: src_ref[0, 0, r], sub * SUB, SUB, xp_ref,
                             lambda r: xg_ref.at[pl.ds(pl.multiple_of(r * X_STRIDE, STRIDE_PAD), PACK_CHUNKS)],
                             sem, PACK_CHUNKS)
        for sub in range(BM // SUB):
            @pl.when(sub < n_sub)
            def _():
                rows = pl.ds(sub * sub_words, sub_words)
                pltpu.make_async_copy(xp_ref.at[rows], xg_ref.at[rows], sem).wait()
        half = D_MODEL // 2
        for sub in range(BM // SUB):
            @pl.when(sub < n_sub)
            def _():
                rows = slice(sub * SUB, (sub + 1) * SUB)
                for c in range(PACK_CHUNKS):
                    w = xg_ref[pl.ds(sub * SUB * X_STRIDE + c, SUB, stride=X_STRIDE), :]
                    xs_ref[rows, c * LANES:(c + 1) * LANES] = pltpu.bitcast(w & jnp.uint32(HI_MASK), F32).astype(BF16)
                    xs_ref[rows, half + c * LANES:half + (c + 1) * LANES] = pltpu.bitcast(w << 16, F32).astype(BF16)

    @pl.when(cnt > 0)
    def _():
        wg = wg_ref[0, 0].astype(BF16)
        wu = wu_ref[0, 0].astype(BF16)
        for sub in range(BM // SUB):
            @pl.when(sub * SUB < cnt)
            def _():
                x = xs_ref[sub * SUB:(sub + 1) * SUB, :]
                g = _dot(x, wg)
                u = _dot(x, wu)
                o_ref[sub * SUB:(sub + 1) * SUB, :] = (g * (1.0 / (1.0 + jnp.exp(-g))) * u).astype(o_ref.dtype)


def _ffn_out_body(be_ref, bx_ref, cnt_ref, h_ref, w_ref, o_ref):
    del be_ref, bx_ref
    cnt = cnt_ref[pl.program_id(0)]
    n_chunks = o_ref.shape[1]
    o2 = o_ref.reshape(BM * n_chunks, LANES)

    @pl.when(cnt > 0)
    def _():
        wb = w_ref[0, 0].astype(BF16)
        for sub in range(BM // SUB):
            @pl.when(sub * SUB < cnt)
            def _():
                y = _dot(h_ref[sub * SUB:(sub + 1) * SUB, :], wb)
                for c in range(n_chunks):
                    o2[pl.ds(sub * SUB * n_chunks + c, SUB, stride=n_chunks), :] = y[:, c * LANES:(c + 1) * LANES]


def expert_ffn(xp, plan, w_in, w_out, layer, tn_in=256, tn_out=1024):
    n_rows = plan["n_rows"]
    nb = plan["n_blocks"]
    d = D_MODEL
    nci = D_EXPERT // tn_in

    def col(c, b, cnt):
        return jnp.where(cnt[b] > 0, c, nci - 1)

    h_mid = pl.pallas_call(
        _ffn_in_body,
        grid_spec=pltpu.PrefetchScalarGridSpec(
            num_scalar_prefetch=3, grid=(nb, nci),
            in_specs=[pl.BlockSpec((1, 1, BM), lambda b, c, be, bx, cnt: (bx[b], 0, 0), memory_space=pltpu.SMEM),
                      pl.BlockSpec(memory_space=pl.ANY),
                      pl.BlockSpec((1, 1, d, tn_in), lambda b, c, be, bx, cnt: (layer, be[b], 0, col(c, b, cnt))),
                      pl.BlockSpec((1, 1, d, tn_in), lambda b, c, be, bx, cnt: (layer, be[b], 0, nci + col(c, b, cnt)))],
            out_specs=pl.BlockSpec((BM, tn_in), lambda b, c, be, bx, cnt: (bx[b], col(c, b, cnt))),
            scratch_shapes=[pltpu.VMEM((BM * X_STRIDE, LANES), jnp.uint32), pltpu.VMEM((BM, d), BF16),
                            pltpu.SemaphoreType.DMA(())]),
        out_shape=jax.ShapeDtypeStruct((n_rows, D_EXPERT), BF16),
        compiler_params=_cparams("arbitrary", "arbitrary"), name="ffn_in")(
            plan["be"], plan["bx"], plan["cnt"], plan["row_src"].reshape(nb, 1, BM), xp, w_in, w_in)
    nco = d // tn_out
    oc = tn_out // LANES

    def ocol(c, b, cnt):
        return jnp.where(cnt[b] > 0, c, nco - 1)

    return pl.pallas_call(
        _ffn_out_body,
        grid_spec=pltpu.PrefetchScalarGridSpec(
            num_scalar_prefetch=3, grid=(nb, nco),
            in_specs=[pl.BlockSpec((BM, D_EXPERT), lambda b, c, be, bx, cnt: (bx[b], 0)),
                      pl.BlockSpec((1, 1, D_EXPERT, tn_out), lambda b, c, be, bx, cnt: (layer, be[b], 0, ocol(c, b, cnt)))],
            out_specs=pl.BlockSpec((BM, oc, LANES), lambda b, c, be, bx, cnt: (bx[b], ocol(c, b, cnt), 0))),
        out_shape=jax.ShapeDtypeStruct((n_rows, ROW_CHUNKS, LANES), F32),
        compiler_params=_cparams("arbitrary", "arbitrary"), name="ffn_out")(
            plan["be"], plan["bx"], plan["cnt"], h_mid, w_out)


def _stream_cols(refs, cols):
    if len(refs) == 1:
        return refs[0][:, cols]
    tr = refs[0].shape[0]
    base = jnp.where(pl.program_id(0) < MP // tr, refs[0][:, cols], refs[1][:, cols])
    return base + refs[2][:, cols] if len(refs) == 3 else base


def _combine_body(dcur_ref, dnext_ref, *refs, n_src, n_norm, split_out, n_tiles):
    src_refs = refs[:n_src]
    ys_ref, gate_ref = refs[n_src:n_src + 2]
    rest = refs[n_src + 2:]
    g_ref = None
    if n_norm:
        g_ref, *rest = rest
    n_out = 2 if split_out else 1
    out_refs = rest[:n_out]
    norm_refs = rest[n_out:n_out + n_norm]
    y_ref, sem = rest[n_out + n_norm:]
    i = pl.program_id(0)
    tr = gate_ref.shape[0]
    tile_rows = tr * ROW_CHUNKS
    buf = jnp.bitwise_and(i, 1)

    def fetch(d_ref, s):
        _gather_rows(lambda e: d_ref[0, 0, e], 0, 2 * tr, ys_ref,
                     lambda e: y_ref.at[s, e & 1, pl.ds(pl.multiple_of((e >> 1) * Y_STRIDE, STRIDE_PAD), ROW_CHUNKS)],
                     sem.at[s], ROW_CHUNKS)

    @pl.when(i == 0)
    def _():
        fetch(dcur_ref, 0)

    @pl.when(i + 1 < n_tiles)
    def _():
        fetch(dnext_ref, 1 - buf)

    for slot in range(2):
        pltpu.make_async_copy(ys_ref.at[pl.ds(0, tile_rows)], y_ref.at[buf, slot, pl.ds(0, tile_rows)], sem.at[buf]).wait()
    g0 = gate_ref[:, 0:1]
    g1 = gate_ref[:, 1:2]

    def emit(o_ref):
        ss = jnp.zeros((tr, 1), F32)
        for c in range(ROW_CHUNKS):
            cols = slice(c * LANES, (c + 1) * LANES)
            moe = (g0 * y_ref[buf, 0, pl.ds(c, tr, stride=Y_STRIDE), :]
                   + g1 * y_ref[buf, 1, pl.ds(c, tr, stride=Y_STRIDE), :])
            hn = _stream_cols(src_refs, cols) + moe
            o_ref[:, cols] = hn
            ss = ss + jnp.sum(hn * hn, axis=1, keepdims=True)
        if n_norm:
            r = lax.rsqrt(ss * (1.0 / D_MODEL) + RMS_EPS)
            for c in range(ROW_CHUNKS):
                cols = slice(c * LANES, (c + 1) * LANES)
                xr = o_ref[:, cols] * r
                for j, n_ref in enumerate(norm_refs):
                    n_ref[:, cols] = (xr * g_ref[j:j + 1, cols]).astype(n_ref.dtype)

    if split_out:
        pl.when(i < n_tiles - 1)(lambda: emit(out_refs[0]))
        pl.when(i == n_tiles - 1)(lambda: emit(out_refs[1]))
    else:
        emit(out_refs[0])


def combine(src, y_sorted, dest, gate, gains, tr, split_out):
    m, d = M_ALL, D_MODEL
    n_tiles = m // tr
    n_norm = 0 if gains is None else gains.shape[0]
    row = pl.BlockSpec((tr, d), lambda i: (i, 0))
    dest3 = dest.reshape(n_tiles, 1, 2 * tr)
    in_specs = [pl.BlockSpec((1, 1, 2 * tr), lambda i: (i, 0, 0), memory_space=pltpu.SMEM),
                pl.BlockSpec((1, 1, 2 * tr), lambda i: (jnp.minimum(i + 1, n_tiles - 1), 0, 0), memory_space=pltpu.SMEM)]
    in_specs += _stream_specs(src, tr) + [pl.BlockSpec(memory_space=pl.ANY), pl.BlockSpec((tr, LANES), lambda i: (i, 0))]
    args = [dest3, dest3, *src, y_sorted, gate]
    if n_norm:
        in_specs.append(pl.BlockSpec((n_norm, d), lambda i: (0, 0)))
        args.append(gains)
    if split_out:
        assert n_norm == 0 and tr == MS
        out_specs = [pl.BlockSpec((tr, d), lambda i: (jnp.minimum(i, n_tiles - 2), 0)), pl.BlockSpec((tr, d), lambda i: (0, 0))]
        out_shape = [jax.ShapeDtypeStruct((MP, d), F32), jax.ShapeDtypeStruct((MS, d), F32)]
    else:
        out_specs = [row] * (1 + n_norm)
        out_shape = [jax.ShapeDtypeStruct((m, d), F32)] + [jax.ShapeDtypeStruct((m, d), BF16)] * n_norm
    return pl.pallas_call(
        functools.partial(_combine_body, n_src=len(src), n_norm=n_norm, split_out=split_out, n_tiles=n_tiles),
        grid=(n_tiles,), in_specs=in_specs, out_specs=out_specs, out_shape=out_shape,
        scratch_shapes=[pltpu.VMEM((2, 2, tr * Y_STRIDE, LANES), F32), pltpu.SemaphoreType.DMA((2,))],
        compiler_params=_cparams("arbitrary"), name="combine")(*args)


def hier_moe_layer(src, g_norm, w_group, w_router, w_in, w_out, layer, next_gains, tr, split_out=False):
    xp, idx, gate = router(src, g_norm, w_group, w_router, tr)
    plan = _moe_plan(idx)
    y_sorted = expert_ffn(xp, plan, w_in, w_out, layer)
    return combine(src, y_sorted.reshape(plan["n_rows"] * ROW_CHUNKS, LANES), plan["dest"], gate, next_gains, tr,
                   split_out)


TM = 640
TR = 128


def kernel(x_prompt, x_sample, state_C, state_n, state_m, cache_k, cache_v, page_table, norm_mix, w_in_a, b_gate_a,
           g_hnorm_a, w_out_a, g_kv, w_kv, g_knorm, w_q_b, g_qnorm_b, b_logit_b, w_out_b, norm_ffn, w_group,
           w_router, w_moe_in, w_moe_out):
    xs = jnp.pad(x_sample, ((0, 0), (0, S_PAD - DEC_SEQ), (0, 0))).reshape(MS, D_MODEL)
    x_in = (x_prompt.reshape(MP, D_MODEL), xs)

    (xn0,) = rms_norm(x_in, norm_mix[0:1], [BF16], TR)
    w_in_t = jnp.swapaxes(w_in_a[0], 0, 1)
    p_main = matmul(xn0, w_in_t, A_MAIN, F32, TM, 512, w_transposed=True)
    gates = mlstm_gates(xn0, w_in_t, b_gate_a[0], TM)
    hg, pc, pn, pm = mlstm(p_main, gates, g_hnorm_a[0], None, 0, BATCH, SEQ, SEQ, 256, 256, MLSTM_SUB)
    hg, sc, sn, sm = mlstm(p_main, gates, g_hnorm_a[0], hg, MP, DEC_BATCH, S_PAD, DEC_SEQ, S_PAD, LANES, LANES,
                           init=(state_C[0], state_n[0], state_m[0]))
    y_mix = matmul(hg, w_out_a[0], D_MODEL, F32, TM, 512)
    h1, xn1, xkv = hier_moe_layer(x_in + (y_mix,), norm_ffn[0], w_group[0], w_router[0], w_moe_in, w_moe_out, 0,
                                  jnp.stack([norm_mix[1], g_kv]), TR)

    kv_all = matmul(xkv, w_kv, 2 * KV_HEADS * HEAD_DIM, F32, TM, 512, mode="headnorm", gain=g_knorm,
                    n_norm_tiles=KV_HEADS * HEAD_DIM // 512)
    q_all = matmul(xn1, w_q_b[0], B_HEADS * HEAD_DIM, BF16, TM, 512, mode="headnorm", gain=g_qnorm_b[0],
                   n_norm_tiles=B_HEADS * HEAD_DIM // 512)
    o_all = attn_prompt(q_all, kv_all, b_logit_b[0])
    o_all = attn_sample(q_all, kv_all, cache_k, cache_v, page_table, b_logit_b[0], o_all)
    h2 = matmul(o_all, w_out_b[0], D_MODEL, F32, TM, 512, mode="resid", resid=h1)
    y_p, y_s = hier_moe_layer((h2,), norm_ffn[1], w_group[1], w_router[1], w_moe_in, w_moe_out, 1, None, TR,
                              split_out=True)

    def sample_rows(a):
        return a[MP:].reshape((DEC_BATCH, S_PAD) + a.shape[1:])[:, :DEC_SEQ]

    kvw = KV_HEADS * HEAD_DIM
    y_prompt = y_p.reshape(BATCH, SEQ, D_MODEL)
    y_sample = y_s.reshape(DEC_BATCH, S_PAD, D_MODEL)[:, :DEC_SEQ]
    prompt_k = kv_all[:MP, :kvw].reshape(BATCH, SEQ, KV_HEADS, HEAD_DIM)
    prompt_v = kv_all[:MP, kvw:].reshape(BATCH, SEQ, KV_HEADS, HEAD_DIM)
    sample_k = sample_rows(kv_all[:, :kvw]).reshape(DEC_BATCH, DEC_SEQ, KV_HEADS, HEAD_DIM)
    sample_v = sample_rows(kv_all[:, kvw:]).reshape(DEC_BATCH, DEC_SEQ, KV_HEADS, HEAD_DIM)
    return (y_prompt, y_sample, pc[None], pn[None], pm[None], prompt_k, prompt_v,
            sc[None], sn[None], sm[None], sample_k, sample_v)
```

```python
import functools

import jax
import jax.numpy as jnp
from jax import lax
from jax.experimental import pallas as pl
from jax.experimental.pallas import tpu as pltpu

F32 = jnp.float32
BF16 = jnp.bfloat16

D_MODEL = 4096
BATCH = 4
SEQ = 2048
DEC_BATCH = 8
DEC_SEQ = 4
PAGE_SIZE = 128
A_HEADS = 8
A_QK_DIM = 256
A_V_DIM = 512
A_QK = A_HEADS * A_QK_DIM
A_VD = A_HEADS * A_V_DIM
A_MAIN = 2 * A_QK + A_VD + D_MODEL
HEAD_DIM = 128
B_HEADS = 32
KV_HEADS = 8
Q_PER_KV = B_HEADS // KV_HEADS
N_GROUPS = 4
EXPERTS_PER_GROUP = 8
N_EXPERTS = N_GROUPS * EXPERTS_PER_GROUP
D_EXPERT = 1024
RMS_EPS = 1e-6

LANES = 128
MP = BATCH * SEQ
S_PAD = 16
MS = DEC_BATCH * S_PAD
M_ALL = MP + MS
NEG_BIG = -1e30
VMEM_LIMIT = 56 * 1024 * 1024

HIGHEST = lax.Precision.HIGHEST


def _cparams(*sem):
    return pltpu.CompilerParams(dimension_semantics=sem, vmem_limit_bytes=VMEM_LIMIT)


def _dot(a, b, precision=None):
    return jnp.dot(a, b, preferred_element_type=F32, precision=precision)


def _dot_nt(a, b):
    return lax.dot_general(a, b, (((1,), (1,)), ((), ())), preferred_element_type=F32)


def _log_sigmoid(x):
    return jnp.minimum(x, 0.0) - jnp.log1p(jnp.exp(-jnp.abs(x)))


def _stream_specs(src, tr):
    row = pl.BlockSpec((tr, D_MODEL), lambda i: (i, 0))
    if len(src) == 1:
        return [row]
    assert tr == MS and MP % tr == 0
    last_prompt = MP // tr - 1
    return [pl.BlockSpec((tr, D_MODEL), lambda i: (jnp.minimum(i, last_prompt), 0)),
            pl.BlockSpec((tr, D_MODEL), lambda i: (0, 0))] + [row] * (len(src) - 2)


def _stream_tile(refs):
    if len(refs) == 1:
        return refs[0][...]
    tr = refs[0].shape[0]
    base = jnp.where(pl.program_id(0) < MP // tr, refs[0][...], refs[1][...])
    return base + refs[2][...] if len(refs) == 3 else base


def _rms_body(*refs, n_src):
    g_ref, *out_refs = refs[n_src:]
    x = _stream_tile(refs[:n_src])
    xr = x * lax.rsqrt(jnp.mean(x * x, axis=-1, keepdims=True) + RMS_EPS)
    for j, o_ref in enumerate(out_refs):
        o_ref[...] = (xr * g_ref[j:j + 1, :]).astype(o_ref.dtype)


def rms_norm(src, gains, out_dtypes, tr):
    n = len(out_dtypes)
    row = pl.BlockSpec((tr, D_MODEL), lambda i: (i, 0))
    return pl.pallas_call(
        functools.partial(_rms_body, n_src=len(src)), grid=(M_ALL // tr,),
        in_specs=_stream_specs(src, tr) + [pl.BlockSpec((n, D_MODEL), lambda i: (0, 0))],
        out_specs=[row] * n,
        out_shape=[jax.ShapeDtypeStruct((M_ALL, D_MODEL), dt) for dt in out_dtypes],
        compiler_params=_cparams("arbitrary"), name="rms_norm")(*src, gains)


def _mm_body(x_ref, w_ref, *rest, mode, n_norm_tiles, w_transposed):
    if mode == "resid":
        r_ref, o_ref, wb_ref = rest
    elif mode == "headnorm":
        g_ref, o_ref, wb_ref = rest
    else:
        o_ref, wb_ref = rest
    j = pl.program_id(0)

    @pl.when(pl.program_id(1) == 0)
    def _():
        wb_ref[...] = w_ref[...].astype(BF16)

    acc = _dot_nt(x_ref[...], wb_ref[...]) if w_transposed else _dot(x_ref[...], wb_ref[...])
    if mode == "plain":
        o_ref[...] = acc.astype(o_ref.dtype)
    elif mode == "resid":
        o_ref[...] = r_ref[...] + acc
    else:
        @pl.when(j < n_norm_tiles)
        def _():
            for c in range(acc.shape[1] // HEAD_DIM):
                a = acc[:, c * HEAD_DIM:(c + 1) * HEAD_DIM]
                y = a * lax.rsqrt(jnp.mean(a * a, axis=-1, keepdims=True) + RMS_EPS) * g_ref[...]
                o_ref[:, c * HEAD_DIM:(c + 1) * HEAD_DIM] = y.astype(o_ref.dtype)

        @pl.when(j >= n_norm_tiles)
        def _():
            o_ref[...] = acc.astype(o_ref.dtype)


def matmul(x, w, n_out, out_dtype, tm, tn, mode="plain", resid=None, gain=None, n_norm_tiles=0, w_transposed=False):
    m, k = x.shape
    w_spec = pl.BlockSpec((tn, k), lambda j, i: (j, 0)) if w_transposed else pl.BlockSpec((k, tn), lambda j, i: (0, j))
    in_specs = [pl.BlockSpec((tm, k), lambda j, i: (i, 0)), w_spec]
    args = [x, w]
    if mode == "resid":
        in_specs.append(pl.BlockSpec((tm, tn), lambda j, i: (i, j)))
        args.append(resid)
    elif mode == "headnorm":
        in_specs.append(pl.BlockSpec((1, HEAD_DIM), lambda j, i: (0, 0)))
        args.append(gain.reshape(1, HEAD_DIM))
    return pl.pallas_call(
        functools.partial(_mm_body, mode=mode, n_norm_tiles=n_norm_tiles, w_transposed=w_transposed),
        grid=(n_out // tn, m // tm),
        in_specs=in_specs,
        out_specs=pl.BlockSpec((tm, tn), lambda j, i: (i, j)),
        out_shape=jax.ShapeDtypeStruct((m, n_out), out_dtype),
        scratch_shapes=[pltpu.VMEM((tn, k) if w_transposed else (k, tn), BF16)],
        compiler_params=_cparams("arbitrary", "arbitrary"), name="matmul_" + mode)(*args)


def _gates_body(x_ref, w_ref, b_ref, o_ref):
    n_gate, k = w_ref.shape
    w = jnp.concatenate([w_ref[...], jnp.zeros((LANES - n_gate, k), F32)], axis=0).astype(BF16)
    pre = _dot_nt(x_ref[...], w) + b_ref[...]
    lane = lax.broadcasted_iota(jnp.int32, pre.shape, 1)
    o_ref[...] = jnp.where(lane < A_HEADS, pre, jnp.where(lane < 2 * A_HEADS, _log_sigmoid(pre), 0.0))


def mlstm_gates(x, w_in_t, b_gate, tm):
    m, k = x.shape
    n_gate = 2 * A_HEADS
    assert A_MAIN % n_gate == 0 and w_in_t.shape[0] - A_MAIN == n_gate
    b = jnp.pad(b_gate.astype(F32), (0, LANES - b_gate.shape[0])).reshape(1, LANES)
    return pl.pallas_call(
        _gates_body, grid=(m // tm,),
        in_specs=[pl.BlockSpec((tm, k), lambda i: (i, 0)), pl.BlockSpec((n_gate, k), lambda i: (A_MAIN // n_gate, 0)),
                  pl.BlockSpec((1, LANES), lambda i: (0, 0))],
        out_specs=pl.BlockSpec((tm, LANES), lambda i: (i, 0)),
        out_shape=jax.ShapeDtypeStruct((m, LANES), F32),
        compiler_params=_cparams("arbitrary"), name="mlstm_gates")(x, w_in_t, b)


MLSTM_SUB = 64
def _mlstm_body(*refs, chunk, lb, sub, t_valid, t_total, has_init):
    if has_init:
        q_ref, k_ref, v_ref, o_ref, ga_ref, gh_ref, c0_ref, n0_ref, m0_ref, hg_ref, c_ref, n_ref, m_ref = refs
    else:
        q_ref, k_ref, v_ref, o_ref, ga_ref, gh_ref, hg_ref, c_ref, n_ref, m_ref = refs
    h = pl.program_id(1)
    c = pl.program_id(2)

    @pl.when(c == 0)
    def _():
        if has_init:
            c_ref[...] = c0_ref[...]
            n_ref[...] = n0_ref[...]
            m_ref[...] = m0_ref[...]
        else:
            c_ref[...] = jnp.zeros_like(c_ref)
            n_ref[...] = jnp.zeros_like(n_ref)
            m_ref[...] = jnp.zeros_like(m_ref)

    def load(ref):
        x = ref[...]
        if lb == chunk:
            return x
        return jnp.concatenate([x, jnp.zeros((chunk - lb, x.shape[1]), x.dtype)], axis=0)

    q = load(q_ref).astype(BF16)
    k_f32 = load(k_ref)
    k = k_f32.astype(BF16)
    v = load(v_ref).astype(BF16)
    ga = load(ga_ref)
    c_state = c_ref[0, 0]
    n_state = n_ref[0, 0]
    m_state = m_ref[0, 0][:, 0:1]
    n_sub = chunk // sub
    sub_shift = sub.bit_length() - 1

    lane = lax.broadcasted_iota(jnp.int32, (chunk, LANES), 1)
    i_col = jnp.sum(jnp.where(lane == h, ga, 0.0), axis=1, keepdims=True)
    f_col = jnp.sum(jnp.where(lane == h + A_HEADS, ga, 0.0), axis=1, keepdims=True)
    if t_valid < t_total:
        t_idx = lax.broadcasted_iota(jnp.int32, (chunk, 1), 0) + c * chunk
        i_col = jnp.where(t_idx < t_valid, i_col, NEG_BIG)
        f_col = jnp.where(t_idx < t_valid, f_col, 0.0)

    row_i = lax.broadcasted_iota(jnp.int32, (chunk, chunk), 0)
    col_i = lax.broadcasted_iota(jnp.int32, (chunk, chunk), 1)
    causal = (col_i <= row_i) & (jnp.right_shift(row_i, sub_shift) == jnp.right_shift(col_i, sub_shift))
    g2 = jnp.where(lane == 0, i_col, jnp.where(lane == 1, f_col, 0.0))
    cum = _dot(causal.astype(F32), g2, precision=HIGHEST)
    g3 = jnp.where(lane == 1, cum, g2)
    g3t = g3.T
    i_row = g3t[0:1, :]
    b_row = g3t[1:2, :]
    b_col = g3[:, 1:2]
    sub_of_row = jnp.right_shift(lax.broadcasted_iota(jnp.int32, (1, chunk), 1), sub_shift)
    sub_of_col = jnp.right_shift(lax.broadcasted_iota(jnp.int32, (chunk, 1), 0), sub_shift)

    m_at = [m_state]
    b_last = []
    for j in range(n_sub):
        bl = b_col[(j + 1) * sub - 1:(j + 1) * sub, :]
        g = bl - b_col[j * sub:(j + 1) * sub, :] + i_col[j * sub:(j + 1) * sub, :]
        b_last.append(bl)
        m_at.append(jnp.maximum(bl + m_at[j], jnp.max(g, axis=0, keepdims=True)))
    m_col = jnp.concatenate([jnp.broadcast_to(m_at[j], (sub, 1)) for j in range(n_sub)], axis=0)

    dmat = jnp.where(causal, b_col - b_row + i_row, NEG_BIG)
    inter = b_col + m_col
    m_t = jnp.maximum(inter, jnp.max(dmat, axis=1, keepdims=True))
    w_intra = jnp.exp(dmat - m_t)
    w_inter = jnp.exp(inter - m_t)
    scale = A_QK_DIM ** -0.5
    s = _dot_nt(q, k) * scale * w_intra
    intra = _dot(s.astype(BF16), v)
    s_sum = jnp.sum(s, axis=1, keepdims=True)

    kf = k.astype(F32)
    kt = k_f32.T
    qc_parts = []
    qn_parts = []
    c_cur = c_state
    n_cur = n_state
    for j in range(n_sub):
        qj = q[j * sub:(j + 1) * sub]
        qc_parts.append(_dot(qj, c_cur.astype(BF16)))
        qn_parts.append(jnp.sum(qj.astype(F32) * n_cur.astype(BF16).astype(F32), axis=1, keepdims=True))
        m_new = m_at[j + 1]
        wk_row = jnp.where(sub_of_row == j, jnp.exp(b_last[j] - b_row + i_row - m_new), 0.0)
        wk_col = jnp.where(sub_of_col == j, jnp.exp(b_last[j] - b_col + i_col - m_new), 0.0)
        decay = jnp.exp(b_last[j] + m_at[j] - m_new)
        c_cur = decay * c_cur + _dot((kt * wk_row).astype(BF16), v)
        n_cur = decay * n_cur + jnp.sum(kf * wk_col.astype(BF16).astype(F32), axis=0, keepdims=True)
    num = w_inter * (jnp.concatenate(qc_parts, axis=0) * scale) + intra
    den = w_inter * (jnp.concatenate(qn_parts, axis=0) * scale) + s_sum
    hh = num / jnp.maximum(jnp.abs(den), jnp.exp(-m_t))

    hn = hh * lax.rsqrt(jnp.mean(hh * hh, axis=1, keepdims=True) + RMS_EPS) * gh_ref[0]
    og = o_ref[...].astype(F32)
    out = hn[0:lb] * (1.0 / (1.0 + jnp.exp(-og)))
    hg_ref[...] = out.astype(hg_ref.dtype)

    c_ref[0, 0] = c_cur
    n_ref[0, 0] = n_cur
    m_ref[0, 0] = jnp.broadcast_to(m_at[n_sub], (1, LANES))


def mlstm(p_main, gates, g_hnorm, hg_prev, row0, n_seq, t_total, t_valid, lb, chunk, sub, init=None):
    m = p_main.shape[0]
    nc = t_total // lb
    rb0 = row0 // lb

    def rows(b, h, c):
        return rb0 + b * nc + c

    qk_blocks = A_QK // A_QK_DIM
    in_specs = [
        pl.BlockSpec((lb, A_QK_DIM), lambda b, h, c: (rows(b, h, c), h)),
        pl.BlockSpec((lb, A_QK_DIM), lambda b, h, c: (rows(b, h, c), qk_blocks + h)),
        pl.BlockSpec((lb, A_V_DIM), lambda b, h, c: (rows(b, h, c), 2 * A_QK // A_V_DIM + h)),
        pl.BlockSpec((lb, A_V_DIM), lambda b, h, c: (rows(b, h, c), (2 * A_QK + A_VD) // A_V_DIM + h)),
        pl.BlockSpec((lb, LANES), lambda b, h, c: (rows(b, h, c), 0)),
        pl.BlockSpec((1, 1, A_V_DIM), lambda b, h, c: (h, 0, 0)),
    ]
    args = [p_main, p_main, p_main, p_main, gates, g_hnorm.reshape(A_HEADS, 1, A_V_DIM)]
    state_specs = [
        pl.BlockSpec((1, 1, A_QK_DIM, A_V_DIM), lambda b, h, c: (b, h, 0, 0)),
        pl.BlockSpec((1, 1, 1, A_QK_DIM), lambda b, h, c: (b, h, 0, 0)),
        pl.BlockSpec((1, 1, 1, LANES), lambda b, h, c: (b, h, 0, 0)),
    ]
    has_init = init is not None
    if has_init:
        c0, n0, m0 = init
        in_specs += state_specs
        args += [c0.astype(F32), n0.astype(F32).reshape(n_seq, A_HEADS, 1, A_QK_DIM),
                 jnp.broadcast_to(m0.astype(F32)[:, :, None, None], (n_seq, A_HEADS, 1, LANES))]
    aliases = {}
    if hg_prev is not None:
        in_specs.append(pl.BlockSpec(memory_space=pl.ANY))
        args.append(hg_prev)
        aliases = {len(args) - 1: 0}

    def body(*refs):
        if hg_prev is not None:
            n_in = len(args)
            refs = refs[:n_in - 1] + refs[n_in:]
        _mlstm_body(*refs, chunk=chunk, lb=lb, sub=sub, t_valid=t_valid, t_total=t_total, has_init=has_init)

    hg, c_out, n_out, m_out = pl.pallas_call(
        body, grid=(n_seq, A_HEADS, nc),
        in_specs=in_specs,
        out_specs=[pl.BlockSpec((lb, A_V_DIM), lambda b, h, c: (rows(b, h, c), h))] + state_specs,
        out_shape=[jax.ShapeDtypeStruct((m, A_VD), BF16),
                   jax.ShapeDtypeStruct((n_seq, A_HEADS, A_QK_DIM, A_V_DIM), F32),
                   jax.ShapeDtypeStruct((n_seq, A_HEADS, 1, A_QK_DIM), F32),
                   jax.ShapeDtypeStruct((n_seq, A_HEADS, 1, LANES), F32)],
        input_output_aliases=aliases,
        compiler_params=_cparams("arbitrary", "arbitrary", "arbitrary"), name="mlstm")(*args)
    return hg, c_out, n_out.reshape(n_seq, A_HEADS, A_QK_DIM), m_out[:, :, 0, 0]


def _sb_block(z, tri, carry, mask):
    l1p = jnp.log(1.0 + jnp.exp(-jnp.abs(z)))
    ls = jnp.minimum(z, 0.0) - l1p
    u = ls - z
    if mask is not None:
        u = jnp.where(mask, u, 0.0)
    u_hi = u.astype(BF16)
    u_lo = (u - u_hi.astype(F32)).astype(BF16)
    within = _dot(u_hi, tri) + _dot(u_lo, tri)
    a = jnp.exp(ls + within + carry)
    if mask is not None:
        a = jnp.where(mask, a, 0.0)
    return a, carry + within[:, 0:1] + u[:, 0:1]


def _strict_tri(tk):
    r = lax.broadcasted_iota(jnp.int32, (tk, tk), 0)
    c = lax.broadcasted_iota(jnp.int32, (tk, tk), 1)
    return (r > c).astype(BF16)


def _attn_prompt_body(q_ref, k_ref, v_ref, b_ref, o_ref, *, tq, tk):
    qi = pl.program_id(2)
    scale = HEAD_DIM ** -0.5
    tri = _strict_tri(tk)
    t_loc = lax.broadcasted_iota(jnp.int32, (tq, tk), 0)
    s_loc = lax.broadcasted_iota(jnp.int32, (tq, tk), 1)
    nkb = tq // tk
    rows = Q_PER_KV * tq
    qs = jnp.concatenate([q_ref[:, g * HEAD_DIM:(g + 1) * HEAD_DIM] for g in range(Q_PER_KV)], axis=0)
    bias = jnp.concatenate([jnp.broadcast_to(b_ref[0, :, g:g + 1], (tq, 1)) for g in range(Q_PER_KV)], axis=0)

    def visit(j, state, diag_off):
        carry, acc = state
        start = pl.multiple_of(j * tk, tk)
        kb = k_ref[pl.ds(start, tk), :].astype(BF16)
        vb = v_ref[pl.ds(start, tk), :].astype(BF16)
        z = _dot_nt(qs, kb) * scale + bias
        mask = None
        if diag_off is not None:
            mask = jnp.concatenate([(s_loc + diag_off * tk) < t_loc] * Q_PER_KV, axis=0)
        a, carry = _sb_block(z, tri, carry, mask)
        return carry, acc + _dot(a.astype(BF16), vb)

    state = (jnp.zeros((rows, 1), F32), jnp.zeros((rows, HEAD_DIM), F32))
    for d in reversed(range(nkb)):
        state = visit(qi * nkb + d, state, d)
    n_before = qi * nkb
    state = lax.fori_loop(0, n_before, lambda it, st: visit(n_before - 1 - it, st, None), state)
    for g in range(Q_PER_KV):
        o_ref[:, g * HEAD_DIM:(g + 1) * HEAD_DIM] = state[1][g * tq:(g + 1) * tq].astype(o_ref.dtype)


def attn_prompt(q_all, kv_all, b_logit, tq=512, tk=128):
    m = q_all.shape[0]
    nq = SEQ // tq
    gw = Q_PER_KV * HEAD_DIM
    bias = b_logit.astype(F32).reshape(KV_HEADS, 1, Q_PER_KV)
    return pl.pallas_call(
        functools.partial(_attn_prompt_body, tq=tq, tk=tk),
        grid=(BATCH, KV_HEADS, nq),
        in_specs=[pl.BlockSpec((tq, gw), lambda b, h, i: (b * nq + i, h)),
                  pl.BlockSpec((SEQ, HEAD_DIM), lambda b, h, i: (b, h)),
                  pl.BlockSpec((SEQ, HEAD_DIM), lambda b, h, i: (b, KV_HEADS + h)),
                  pl.BlockSpec((1, 1, Q_PER_KV), lambda b, h, i: (h, 0, 0))],
        out_specs=pl.BlockSpec((tq, gw), lambda b, h, i: (b * nq + i, h)),
        out_shape=jax.ShapeDtypeStruct((m, B_HEADS * HEAD_DIM), BF16),
        compiler_params=_cparams("arbitrary", "arbitrary", "arbitrary"), name="attn_prompt")(q_all, kv_all, kv_all, bias)


QROWS = 8
PAGES_PER_STEP = 8


def _attn_sample_body(pt_ref, q_ref, kn_ref, vn_ref, *rest, n_steps):
    ck_refs = rest[:PAGES_PER_STEP]
    cv_refs = rest[PAGES_PER_STEP:2 * PAGES_PER_STEP]
    b_ref, oin_ref, o_ref, qs_ref, car_ref, acc_ref = rest[2 * PAGES_PER_STEP:]
    del pt_ref, oin_ref
    p = pl.program_id(1)
    scale = HEAD_DIM ** -0.5
    rows_h = Q_PER_KV * QROWS
    tri = _strict_tri(PAGE_SIZE)

    def process(get_k, get_v, mask, car, acc):
        qs = qs_ref[...]
        zs = [_dot_nt(qs[kh * rows_h:(kh + 1) * rows_h].astype(BF16), get_k(kh).astype(BF16)) for kh in range(KV_HEADS)]
        z = jnp.concatenate(zs, axis=0) * scale + b_ref[...]
        a, car = _sb_block(z, tri, car, mask)
        ab = a.astype(BF16)
        accs = [acc[kh * rows_h:(kh + 1) * rows_h] + _dot(ab[kh * rows_h:(kh + 1) * rows_h], get_v(kh).astype(BF16))
                for kh in range(KV_HEADS)]
        return car, jnp.concatenate(accs, axis=0)

    @pl.when(p == 0)
    def _():
        for hq in range(B_HEADS):
            qs_ref[hq * QROWS:(hq + 1) * QROWS, :] = q_ref[:, hq * HEAD_DIM:(hq + 1) * HEAD_DIM].astype(F32)[0:QROWS]
        pad = jnp.zeros((PAGE_SIZE - S_PAD, HEAD_DIM), F32)
        n_rows = B_HEADS * QROWS
        t_loc = jnp.bitwise_and(lax.broadcasted_iota(jnp.int32, (n_rows, PAGE_SIZE), 0), QROWS - 1)
        s_loc = lax.broadcasted_iota(jnp.int32, (n_rows, PAGE_SIZE), 1)
        car, acc = process(lambda kh: jnp.concatenate([kn_ref[:, kh * HEAD_DIM:(kh + 1) * HEAD_DIM], pad], axis=0),
                           lambda kh: jnp.concatenate([vn_ref[:, kh * HEAD_DIM:(kh + 1) * HEAD_DIM], pad], axis=0),
                           s_loc < t_loc, jnp.zeros((n_rows, 1), F32), jnp.zeros((n_rows, HEAD_DIM), F32))
        car_ref[...] = car
        acc_ref[...] = acc

    car = car_ref[...]
    acc = acc_ref[...]
    for ck_ref, cv_ref in zip(ck_refs, cv_refs):
        car, acc = process(lambda kh: ck_ref[0, pl.ds(kh, PAGE_SIZE, stride=KV_HEADS), :],
                           lambda kh: cv_ref[0, pl.ds(kh, PAGE_SIZE, stride=KV_HEADS), :], None, car, acc)
    car_ref[...] = car
    acc_ref[...] = acc

    @pl.when(p == n_steps - 1)
    def _():
        pad = jnp.zeros((S_PAD - QROWS, HEAD_DIM), F32)
        for hq in range(B_HEADS):
            full = jnp.concatenate([acc[hq * QROWS:(hq + 1) * QROWS, :], pad], axis=0)
            o_ref[:, hq * HEAD_DIM:(hq + 1) * HEAD_DIM] = full.astype(o_ref.dtype)


def attn_sample(q_all, kv_all, cache_k, cache_v, page_table, b_logit, o_prev):
    n_seq, n_pages = page_table.shape
    n_steps = n_pages // PAGES_PER_STEP
    rb0 = MP // S_PAD
    kvw = KV_HEADS * HEAD_DIM
    n_phys = cache_k.shape[0]
    ck = cache_k.reshape(n_phys, PAGE_SIZE * KV_HEADS, HEAD_DIM)
    cv = cache_v.reshape(n_phys, PAGE_SIZE * KV_HEADS, HEAD_DIM)
    bias = jnp.broadcast_to(b_logit.astype(F32)[:, None, None], (B_HEADS, QROWS, 1)).reshape(B_HEADS * QROWS, 1)

    def page(u):
        return lambda s, p, pt: (pt[s * n_pages + n_pages - 1 - (p * PAGES_PER_STEP + u)], 0, 0)

    page_specs = [pl.BlockSpec((1, PAGE_SIZE * KV_HEADS, HEAD_DIM), page(u)) for u in range(PAGES_PER_STEP)]
    grid_spec = pltpu.PrefetchScalarGridSpec(
        num_scalar_prefetch=1, grid=(n_seq, n_steps),
        in_specs=[pl.BlockSpec((S_PAD, B_HEADS * HEAD_DIM), lambda s, p, pt: (rb0 + s, 0)),
                  pl.BlockSpec((S_PAD, kvw), lambda s, p, pt: (rb0 + s, 0)),
                  pl.BlockSpec((S_PAD, kvw), lambda s, p, pt: (rb0 + s, 1))] + page_specs + page_specs + [
                  pl.BlockSpec((B_HEADS * QROWS, 1), lambda s, p, pt: (0, 0)),
                  pl.BlockSpec(memory_space=pl.ANY)],
        out_specs=pl.BlockSpec((S_PAD, B_HEADS * HEAD_DIM), lambda s, p, pt: (rb0 + s, 0)),
        scratch_shapes=[pltpu.VMEM((B_HEADS * QROWS, HEAD_DIM), F32),
                        pltpu.VMEM((B_HEADS * QROWS, 1), F32),
                        pltpu.VMEM((B_HEADS * QROWS, HEAD_DIM), F32)])
    args = [page_table.reshape(-1), q_all, kv_all, kv_all] + [ck] * PAGES_PER_STEP + [cv] * PAGES_PER_STEP + [bias, o_prev]
    return pl.pallas_call(
        functools.partial(_attn_sample_body, n_steps=n_steps),
        grid_spec=grid_spec,
        out_shape=jax.ShapeDtypeStruct(o_prev.shape, o_prev.dtype),
        input_output_aliases={len(args) - 1: 0},
        compiler_params=_cparams("arbitrary", "arbitrary"), name="attn_sample")(*args)


PACK_CHUNKS = D_MODEL // 2 // LANES
ROW_CHUNKS = D_MODEL // LANES
STRIDE_PAD = 4
X_STRIDE = PACK_CHUNKS + STRIDE_PAD
Y_STRIDE = ROW_CHUNKS + STRIDE_PAD
HI_MASK = 0xFFFF0000


def _router_body(*refs, n_src):
    g_ref, w_ref, xp_ref, idx_ref, gate_ref = refs[n_src:]
    x = _stream_tile(refs[:n_src])
    tr = x.shape[0]
    xn = x * lax.rsqrt(jnp.mean(x * x, axis=-1, keepdims=True) + RMS_EPS) * g_ref[...]
    xb = xn.astype(BF16)
    bits = pltpu.bitcast(xb.astype(F32), jnp.uint32)
    packed = bits[:, :D_MODEL // 2] | (bits[:, D_MODEL // 2:] >> 16)
    for c in range(PACK_CHUNKS):
        xp_ref[pl.ds(c, tr, stride=PACK_CHUNKS), :] = packed[:, c * LANES:(c + 1) * LANES]
    logits = _dot(xb, w_ref[...].astype(BF16))
    lane = lax.broadcasted_iota(jnp.int32, logits.shape, 1)
    lane_f = lane.astype(F32)
    ninf = -jnp.inf

    def first_max(vals):
        mx = jnp.max(vals, axis=1, keepdims=True)
        return mx, jnp.min(jnp.where(vals == mx, lane_f, float(LANES)), axis=1, keepdims=True)

    gl = jnp.where(lane < N_GROUPS, logits, ninf)
    gmax, grp = first_max(gl)
    g_gate = 1.0 / jnp.sum(jnp.exp(gl - gmax), axis=1, keepdims=True)
    e_lane = lane - N_GROUPS
    lane_grp = jnp.right_shift(e_lane, 3).astype(F32)
    in_grp = (e_lane >= 0) & (e_lane < N_EXPERTS) & (lane_grp == grp)
    el = jnp.where(in_grp, logits, ninf)
    t1, i1 = first_max(el)
    t2, i2 = first_max(jnp.where(lane_f == i1, ninf, el))
    e21 = jnp.exp(t2 - t1)
    g1 = g_gate / (1.0 + e21)
    g2 = g_gate * e21 / (1.0 + e21)
    idx_ref[...] = jnp.where(lane == 0, i1, jnp.where(lane == 1, i2, float(N_GROUPS))).astype(jnp.int32) - N_GROUPS
    gate_ref[...] = jnp.where(lane == 0, g1, jnp.where(lane == 1, g2, 0.0))


def router(src, g_norm, w_group, w_router, tr):
    m, d = M_ALL, D_MODEL
    w = jnp.pad(jnp.concatenate([w_group, w_router], axis=1).astype(F32), ((0, 0), (0, LANES - N_GROUPS - N_EXPERTS)))
    small = pl.BlockSpec((tr, LANES), lambda i: (i, 0))
    xp, idx, gate = pl.pallas_call(
        functools.partial(_router_body, n_src=len(src)), grid=(m // tr,),
        in_specs=_stream_specs(src, tr) + [pl.BlockSpec((1, d), lambda i: (0, 0)),
                                           pl.BlockSpec((d, LANES), lambda i: (0, 0))],
        out_specs=[pl.BlockSpec((tr * PACK_CHUNKS, LANES), lambda i: (i, 0)), small, small],
        out_shape=[jax.ShapeDtypeStruct((m * PACK_CHUNKS, LANES), jnp.uint32),
                   jax.ShapeDtypeStruct((m, LANES), jnp.int32), jax.ShapeDtypeStruct((m, LANES), F32)],
        compiler_params=_cparams("arbitrary"), name="router")(*src, g_norm.reshape(1, d), w)
    return xp, idx[:, :2], gate


BM = 1024
SUB = 256


def _moe_plan(idx):
    m = idx.shape[0]
    flat_e = idx.reshape(-1)
    onehot = (flat_e[:, None] == jnp.arange(N_EXPERTS, dtype=jnp.int32)[None, :]).astype(jnp.int32)
    csum = jnp.cumsum(onehot, axis=0)
    rank = jnp.take_along_axis(csum, flat_e[:, None], axis=1)[:, 0] - 1
    counts = csum[-1]
    nblk = (counts + BM - 1) // BM
    ends = jnp.cumsum(nblk)
    bstart = ends - nblk
    dest = (bstart[flat_e] * BM + rank).astype(jnp.int32)
    n_blocks = N_EXPERTS + (2 * m) // BM
    n_used = ends[-1]
    used = jnp.arange(n_blocks) < n_used
    blk = jnp.minimum(jnp.arange(n_blocks, dtype=jnp.int32), n_used - 1)
    be = jnp.minimum(jnp.searchsorted(ends, blk, side="right"), N_EXPERTS - 1).astype(jnp.int32)
    cnt = jnp.clip(counts[be] - (blk - bstart[be]) * BM, 0, BM)
    cnt = jnp.where(used, cnt, 0).astype(jnp.int32)
    n_rows = n_blocks * BM
    tok = jnp.arange(2 * m, dtype=jnp.int32) // 2
    row_src = jnp.zeros((n_rows,), jnp.int32).at[dest].set(tok)
    return dict(be=be, bx=blk.astype(jnp.int32), cnt=cnt, row_src=row_src, dest=dest, n_blocks=n_blocks, n_rows=n_rows)


def _gather_rows(idx_of, first, n, src_ref, dst_of, sem, rs):
    def start(r, carry):
        src_row = pl.multiple_of(idx_of(r) * rs, rs)
        pltpu.make_async_copy(src_ref.at[pl.ds(src_row, rs)], dst_of(r), sem).start()
        return carry

    lax.fori_loop(first, first + n, start, 0, unroll=8)


def _ffn_in_body(be_ref, bx_ref, cnt_ref, src_ref, xp_ref, wg_ref, wu_ref, o_ref, xg_ref, xs_ref, sem):
    del be_ref, bx_ref
    b = pl.program_id(0)
    cnt = cnt_ref[b]
    n_sub = (cnt + SUB - 1) // SUB
    sub_words = SUB * PACK_CHUNKS

    @pl.when((pl.program_id(1) == 0) & (cnt > 0))
    def _():
        for sub in range(BM // SUB):
            @pl.when(sub < n_sub)
            def _():
                _gather_rows(lambda r: src_ref[0, 0, r], sub * SUB, SUB, xp_ref,
                             lambda r: xg_ref.at[pl.ds(pl.multiple_of(r * X_STRIDE, STRIDE_PAD), PACK_CHUNKS)],
                             sem, PACK_CHUNKS)
        for sub in range(BM // SUB):
            @pl.when(sub < n_sub)
            def _():
                rows = pl.ds(sub * sub_words, sub_words)
                pltpu.make_async_copy(xp_ref.at[rows], xg_ref.at[rows], sem).wait()
        half = D_MODEL // 2
        for sub in range(BM // SUB):
            @pl.when(sub < n_sub)
            def _():
                rows = slice(sub * SUB, (sub + 1) * SUB)
                for c in range(PACK_CHUNKS):
                    w = xg_ref[pl.ds(sub * SUB * X_STRIDE + c, SUB, stride=X_STRIDE), :]
                    xs_ref[rows, c * LANES:(c + 1) * LANES] = pltpu.bitcast(w & jnp.uint32(HI_MASK), F32).astype(BF16)
                    xs_ref[rows, half + c * LANES:half + (c + 1) * LANES] = pltpu.bitcast(w << 16, F32).astype(BF16)

    @pl.when(cnt > 0)
    def _():
        wg = wg_ref[0, 0].astype(BF16)
        wu = wu_ref[0, 0].astype(BF16)
        for sub in range(BM // SUB):
            @pl.when(sub * SUB < cnt)
            def _():
                x = xs_ref[sub * SUB:(sub + 1) * SUB, :]
                g = _dot(x, wg)
                u = _dot(x, wu)
                o_ref[sub * SUB:(sub + 1) * SUB, :] = (g * (1.0 / (1.0 + jnp.exp(-g))) * u).astype(o_ref.dtype)


def _ffn_out_body(be_ref, bx_ref, cnt_ref, h_ref, w_ref, o_ref):
    del be_ref, bx_ref
    cnt = cnt_ref[pl.program_id(0)]
    n_chunks = o_ref.shape[1]
    o2 = o_ref.reshape(BM * n_chunks, LANES)

    @pl.when(cnt > 0)
    def _():
        wb = w_ref[0, 0].astype(BF16)
        for sub in range(BM // SUB):
            @pl.when(sub * SUB < cnt)
            def _():
                y = _dot(h_ref[sub * SUB:(sub + 1) * SUB, :], wb)
                for c in range(n_chunks):
                    o2[pl.ds(sub * SUB * n_chunks + c, SUB, stride=n_chunks), :] = y[:, c * LANES:(c + 1) * LANES]


def expert_ffn(xp, plan, w_in, w_out, layer, tn_in=256, tn_out=1024):
    n_rows = plan["n_rows"]
    nb = plan["n_blocks"]
    d = D_MODEL
    nci = D_EXPERT // tn_in

    def col(c, b, cnt):
        return jnp.where(cnt[b] > 0, c, nci - 1)

    h_mid = pl.pallas_call(
        _ffn_in_body,
        grid_spec=pltpu.PrefetchScalarGridSpec(
            num_scalar_prefetch=3, grid=(nb, nci),
            in_specs=[pl.BlockSpec((1, 1, BM), lambda b, c, be, bx, cnt: (bx[b], 0, 0), memory_space=pltpu.SMEM),
                      pl.BlockSpec(memory_space=pl.ANY),
                      pl.BlockSpec((1, 1, d, tn_in), lambda b, c, be, bx, cnt: (layer, be[b], 0, col(c, b, cnt))),
                      pl.BlockSpec((1, 1, d, tn_in), lambda b, c, be, bx, cnt: (layer, be[b], 0, nci + col(c, b, cnt)))],
            out_specs=pl.BlockSpec((BM, tn_in), lambda b, c, be, bx, cnt: (bx[b], col(c, b, cnt))),
            scratch_shapes=[pltpu.VMEM((BM * X_STRIDE, LANES), jnp.uint32), pltpu.VMEM((BM, d), BF16),
                            pltpu.SemaphoreType.DMA(())]),
        out_shape=jax.ShapeDtypeStruct((n_rows, D_EXPERT), BF16),
        compiler_params=_cparams("arbitrary", "arbitrary"), name="ffn_in")(
            plan["be"], plan["bx"], plan["cnt"], plan["row_src"].reshape(nb, 1, BM), xp, w_in, w_in)
    nco = d // tn_out
    oc = tn_out // LANES

    def ocol(c, b, cnt):
        return jnp.where(cnt[b] > 0, c, nco - 1)

    return pl.pallas_call(
        _ffn_out_body,
        grid_spec=pltpu.PrefetchScalarGridSpec(
            num_scalar_prefetch=3, grid=(nb, nco),
            in_specs=[pl.BlockSpec((BM, D_EXPERT), lambda b, c, be, bx, cnt: (bx[b], 0)),
                      pl.BlockSpec((1, 1, D_EXPERT, tn_out), lambda b, c, be, bx, cnt: (layer, be[b], 0, ocol(c, b, cnt)))],
            out_specs=pl.BlockSpec((BM, oc, LANES), lambda b, c, be, bx, cnt: (bx[b], ocol(c, b, cnt), 0))),
        out_shape=jax.ShapeDtypeStruct((n_rows, ROW_CHUNKS, LANES), F32),
        compiler_params=_cparams("arbitrary", "arbitrary"), name="ffn_out")(
            plan["be"], plan["bx"], plan["cnt"], h_mid, w_out)


def _stream_cols(refs, cols):
    if len(refs) == 1:
        return refs[0][:, cols]
    tr = refs[0].shape[0]
    base = jnp.where(pl.program_id(0) < MP // tr, refs[0][:, cols], refs[1][:, cols])
    return base + refs[2][:, cols] if len(refs) == 3 else base


def _combine_body(dcur_ref, dnext_ref, *refs, n_src, n_norm, split_out, n_tiles):
    src_refs = refs[:n_src]
    ys_ref, gate_ref = refs[n_src:n_src + 2]
    rest = refs[n_src + 2:]
    g_ref = None
    if n_norm:
        g_ref, *rest = rest
    n_out = 2 if split_out else 1
    out_refs = rest[:n_out]
    norm_refs = rest[n_out:n_out + n_norm]
    y_ref, sem = rest[n_out + n_norm:]
    i = pl.program_id(0)
    tr = gate_ref.shape[0]
    tile_rows = tr * ROW_CHUNKS
    buf = jnp.bitwise_and(i, 1)

    def fetch(d_ref, s):
        _gather_rows(lambda e: d_ref[0, 0, e], 0, 2 * tr, ys_ref,
                     lambda e: y_ref.at[s, e & 1, pl.ds(pl.multiple_of((e >> 1) * Y_STRIDE, STRIDE_PAD), ROW_CHUNKS)],
                     sem.at[s], ROW_CHUNKS)

    @pl.when(i == 0)
    def _():
        fetch(dcur_ref, 0)

    @pl.when(i + 1 < n_tiles)
    def _():
        fetch(dnext_ref, 1 - buf)

    for slot in range(2):
        pltpu.make_async_copy(ys_ref.at[pl.ds(0, tile_rows)], y_ref.at[buf, slot, pl.ds(0, tile_rows)], sem.at[buf]).wait()
    g0 = gate_ref[:, 0:1]
    g1 = gate_ref[:, 1:2]

    def emit(o_ref):
        ss = jnp.zeros((tr, 1), F32)
        for c in range(ROW_CHUNKS):
            cols = slice(c * LANES, (c + 1) * LANES)
            moe = (g0 * y_ref[buf, 0, pl.ds(c, tr, stride=Y_STRIDE), :]
                   + g1 * y_ref[buf, 1, pl.ds(c, tr, stride=Y_STRIDE), :])
            hn = _stream_cols(src_refs, cols) + moe
            o_ref[:, cols] = hn
            ss = ss + jnp.sum(hn * hn, axis=1, keepdims=True)
        if n_norm:
            r = lax.rsqrt(ss * (1.0 / D_MODEL) + RMS_EPS)
            for c in range(ROW_CHUNKS):
                cols = slice(c * LANES, (c + 1) * LANES)
                xr = o_ref[:, cols] * r
                for j, n_ref in enumerate(norm_refs):
                    n_ref[:, cols] = (xr * g_ref[j:j + 1, cols]).astype(n_ref.dtype)

    if split_out:
        pl.when(i < n_tiles - 1)(lambda: emit(out_refs[0]))
        pl.when(i == n_tiles - 1)(lambda: emit(out_refs[1]))
    else:
        emit(out_refs[0])


def combine(src, y_sorted, dest, gate, gains, tr, split_out):
    m, d = M_ALL, D_MODEL
    n_tiles = m // tr
    n_norm = 0 if gains is None else gains.shape[0]
    row = pl.BlockSpec((tr, d), lambda i: (i, 0))
    dest3 = dest.reshape(n_tiles, 1, 2 * tr)
    in_specs = [pl.BlockSpec((1, 1, 2 * tr), lambda i: (i, 0, 0), memory_space=pltpu.SMEM),
                pl.BlockSpec((1, 1, 2 * tr), lambda i: (jnp.minimum(i + 1, n_tiles - 1), 0, 0), memory_space=pltpu.SMEM)]
    in_specs += _stream_specs(src, tr) + [pl.BlockSpec(memory_space=pl.ANY), pl.BlockSpec((tr, LANES), lambda i: (i, 0))]
    args = [dest3, dest3, *src, y_sorted, gate]
    if n_norm:
        in_specs.append(pl.BlockSpec((n_norm, d), lambda i: (0, 0)))
        args.append(gains)
    if split_out:
        assert n_norm == 0 and tr == MS
        out_specs = [pl.BlockSpec((tr, d), lambda i: (jnp.minimum(i, n_tiles - 2), 0)), pl.BlockSpec((tr, d), lambda i: (0, 0))]
        out_shape = [jax.ShapeDtypeStruct((MP, d), F32), jax.ShapeDtypeStruct((MS, d), F32)]
    else:
        out_specs = [row] * (1 + n_norm)
        out_shape = [jax.ShapeDtypeStruct((m, d), F32)] + [jax.ShapeDtypeStruct((m, d), BF16)] * n_norm
    return pl.pallas_call(
        functools.partial(_combine_body, n_src=len(src), n_norm=n_norm, split_out=split_out, n_tiles=n_tiles),
        grid=(n_tiles,), in_specs=in_specs, out_specs=out_specs, out_shape=out_shape,
        scratch_shapes=[pltpu.VMEM((2, 2, tr * Y_STRIDE, LANES), F32), pltpu.SemaphoreType.DMA((2,))],
        compiler_params=_cparams("arbitrary"), name="combine")(*args)


def hier_moe_layer(src, g_norm, w_group, w_router, w_in, w_out, layer, next_gains, tr, split_out=False):
    xp, idx, gate = router(src, g_norm, w_group, w_router, tr)
    plan = _moe_plan(idx)
    y_sorted = expert_ffn(xp, plan, w_in, w_out, layer)
    return combine(src, y_sorted.reshape(plan["n_rows"] * ROW_CHUNKS, LANES), plan["dest"], gate, next_gains, tr,
                   split_out)


TM = 640
TR = 128


def kernel(x_prompt, x_sample, state_C, state_n, state_m, cache_k, cache_v, page_table, norm_mix, w_in_a, b_gate_a,
           g_hnorm_a, w_out_a, g_kv, w_kv, g_knorm, w_q_b, g_qnorm_b, b_logit_b, w_out_b, norm_ffn, w_group,
           w_router, w_moe_in, w_moe_out):
    xs = jnp.pad(x_sample, ((0, 0), (0, S_PAD - DEC_SEQ), (0, 0))).reshape(MS, D_MODEL)
    x_in = (x_prompt.reshape(MP, D_MODEL), xs)

    (xn0,) = rms_norm(x_in, norm_mix[0:1], [BF16], TR)
    w_in_t = jnp.swapaxes(w_in_a[0], 0, 1)
    p_main = matmul(xn0, w_in_t, A_MAIN, F32, TM, 512, w_transposed=True)
    gates = mlstm_gates(xn0, w_in_t, b_gate_a[0], TM)
    hg, pc, pn, pm = mlstm(p_main, gates, g_hnorm_a[0], None, 0, BATCH, SEQ, SEQ, 256, 256, MLSTM_SUB)
    hg, sc, sn, sm = mlstm(p_main, gates, g_hnorm_a[0], hg, MP, DEC_BATCH, S_PAD, DEC_SEQ, S_PAD, LANES, LANES,
                           init=(state_C[0], state_n[0], state_m[0]))
    y_mix = matmul(hg, w_out_a[0], D_MODEL, F32, TM, 512)
    h1, xn1, xkv = hier_moe_layer(x_in + (y_mix,), norm_ffn[0], w_group[0], w_router[0], w_moe_in, w_moe_out, 0,
                                  jnp.stack([norm_mix[1], g_kv]), TR)

    kv_all = matmul(xkv, w_kv, 2 * KV_HEADS * HEAD_DIM, F32, TM, 512, mode="headnorm", gain=g_knorm,
                    n_norm_tiles=KV_HEADS * HEAD_DIM // 512)
    q_all = matmul(xn1, w_q_b[0], B_HEADS * HEAD_DIM, BF16, TM, 512, mode="headnorm", gain=g_qnorm_b[0],
                   n_norm_tiles=B_HEADS * HEAD_DIM // 512)
    o_all = attn_prompt(q_all, kv_all, b_logit_b[0])
    o_all = attn_sample(q_all, kv_all, cache_k, cache_v, page_table, b_logit_b[0], o_all)
    h2 = matmul(o_all, w_out_b[0], D_MODEL, F32, TM, 512, mode="resid", resid=h1)
    y_p, y_s = hier_moe_layer((h2,), norm_ffn[1], w_group[1], w_router[1], w_moe_in, w_moe_out, 1, None, TR,
                              split_out=True)

    def sample_rows(a):
        return a[MP:].reshape((DEC_BATCH, S_PAD) + a.shape[1:])[:, :DEC_SEQ]

    kvw = KV_HEADS * HEAD_DIM
    y_prompt = y_p.reshape(BATCH, SEQ, D_MODEL)
    y_sample = y_s.reshape(DEC_BATCH, S_PAD, D_MODEL)[:, :DEC_SEQ]
    prompt_k = kv_all[:MP, :kvw].reshape(BATCH, SEQ, KV_HEADS, HEAD_DIM)
    prompt_v = kv_all[:MP, kvw:].reshape(BATCH, SEQ, KV_HEADS, HEAD_DIM)
    sample_k = sample_rows(kv_all[:, :kvw]).reshape(DEC_BATCH, DEC_SEQ, KV_HEADS, HEAD_DIM)
    sample_v = sample_rows(kv_all[:, kvw:]).reshape(DEC_BATCH, DEC_SEQ, KV_HEADS, HEAD_DIM)
    return (y_prompt, y_sample, pc[None], pn[None], pm[None], prompt_k, prompt_v,
            sc[None], sn[None], sm[None], sample_k, sample_v)
```
